```python
import math
import jax, jax.numpy as jnp
from jax import lax
import numpy as np

D_MODEL = 1024
BATCH = 8
SEQ = 8192
DEPTH = 4

N_MIXERS = 2
HEAD_DIM = 64
A_Q_HEADS = 12
A_KV_HEADS = 4
A_GROUP = A_Q_HEADS // A_KV_HEADS
WINDOW = 128
BLOCK = 128
SPAN = BLOCK + 2 * WINDOW
B_HEADS = 12
GRID_W = 64
NB_ROWS_MAX = 8
NB_COLS = 16
MEM_HEADS = 4
MEM_LEN = 256
N_BUCKETS = 32
MAX_DISTANCE = 128
D_FF = -(-8 * D_MODEL // (3 * 256)) * 256
RMS_EPS = 1e-6
A_IN = (A_Q_HEADS + 2 * A_KV_HEADS + MEM_HEADS) * HEAD_DIM
B_IN = (3 * B_HEADS + MEM_HEADS) * HEAD_DIM
MIX_WIDTH = (A_Q_HEADS + MEM_HEADS) * HEAD_DIM
NEG = -1e30

kernel_name = 'hybrid_window_gqa_natten_memory_encoder'


def rms_norm(x, g):
    xf = x.astype(jnp.float32)
    y = xf * lax.rsqrt(jnp.mean(xf * xf, axis=-1, keepdims=True) + RMS_EPS)
    return (y * g.astype(jnp.float32)).astype(x.dtype)


def t5_bucket(rel):
    half = N_BUCKETS // 2
    max_exact = half // 2
    n = -rel
    ret = jnp.where(n < 0, half, 0)
    n = jnp.abs(n)
    nf = jnp.maximum(n, 1).astype(jnp.float32)
    large = max_exact + (jnp.log(nf / max_exact) / math.log(MAX_DISTANCE / max_exact)
                         * (half - max_exact)).astype(jnp.int32)
    large = jnp.minimum(large, half - 1)
    return ret + jnp.where(n < max_exact, n, large)


def windowed_gqa(q, k, v, sink, t5_table):
    B, S = q.shape[0], q.shape[1]
    nblk = S // BLOCK
    scale = HEAD_DIM ** -0.5
    kp = jnp.pad(k, ((0, 0), (WINDOW, WINDOW), (0, 0), (0, 0)))
    vp = jnp.pad(v, ((0, 0), (WINDOW, WINDOW), (0, 0), (0, 0)))
    rel = jnp.arange(SPAN)[None, :] - WINDOW - jnp.arange(BLOCK)[:, None]
    in_window = jnp.abs(rel) <= WINDOW
    bias = t5_table[t5_bucket(rel)].astype(jnp.float32)
    bias = bias.transpose(2, 0, 1).reshape(A_KV_HEADS, A_GROUP, BLOCK, SPAN)
    sink_l = jnp.broadcast_to(sink.astype(jnp.float32).reshape(1, A_KV_HEADS, A_GROUP, 1, 1),
                              (B, A_KV_HEADS, A_GROUP, BLOCK, 1))
    qb = q.reshape(B, nblk, BLOCK, A_KV_HEADS, A_GROUP, HEAD_DIM).transpose(1, 0, 2, 3, 4, 5)

    def one_block(args):
        i, q_blk = args
        start = i * BLOCK
        k_blk = lax.dynamic_slice_in_dim(kp, start, SPAN, axis=1)
        v_blk = lax.dynamic_slice_in_dim(vp, start, SPAN, axis=1)
        kpos = start - WINDOW + jnp.arange(SPAN)
        valid = in_window & ((kpos >= 0) & (kpos < S))[None, :]
        s = jnp.einsum('bqkgd,bskd->bkgqs', q_blk, k_blk,
                       preferred_element_type=jnp.float32) * scale + bias
        s = jnp.where(valid, s, NEG)
        p = jax.nn.softmax(jnp.concatenate([s, sink_l], axis=-1), axis=-1)[..., :SPAN]
        return jnp.einsum('bkgqs,bskd->bqkgd', p.astype(v_blk.dtype), v_blk)

    out = lax.map(one_block, (jnp.arange(nblk), qb))
    return out.transpose(1, 0, 2, 3, 4, 5).reshape(B, S, A_Q_HEADS * HEAD_DIM)


def neighbourhood_attn(q, k, v, rpb):
    B, S = q.shape[0], q.shape[1]
    rows = S // GRID_W
    kr = min(NB_ROWS_MAX, rows)
    kc = NB_COLS
    scale = HEAD_DIM ** -0.5
    qg = q.reshape(B, rows, GRID_W, B_HEADS, HEAD_DIM).transpose(1, 0, 2, 3, 4)
    kg = k.reshape(B, rows, GRID_W, B_HEADS, HEAD_DIM)
    vg = v.reshape(B, rows, GRID_W, B_HEADS, HEAD_DIM)
    row_start = jnp.clip(jnp.arange(rows) - kr // 2, 0, rows - kr)
    c_idx = jnp.arange(GRID_W)
    col_start = jnp.clip(c_idx - kc // 2, 0, GRID_W - kc)
    col_nb = col_start[:, None] + jnp.arange(kc)[None, :]
    col_bias_idx = col_nb - c_idx[:, None] + NB_COLS - 1

    def one_row(args):
        r, q_row = args
        rs = row_start[r]
        k_rows = lax.dynamic_slice_in_dim(kg, rs, kr, axis=1)
        v_rows = lax.dynamic_slice_in_dim(vg, rs, kr, axis=1)
        k_nb = k_rows[:, :, col_nb]
        v_nb = v_rows[:, :, col_nb]
        row_bias_idx = rs + jnp.arange(kr) - r + NB_ROWS_MAX - 1
        bias = rpb[:, row_bias_idx[:, None, None], col_bias_idx[None, :, :]]
        bias = bias.astype(jnp.float32).transpose(0, 2, 1, 3)[None]
        s = jnp.einsum('bwhd,brwchd->bhwrc', q_row, k_nb,
                       preferred_element_type=jnp.float32) * scale + bias
        p = jax.nn.softmax(s.reshape(B, B_HEADS, GRID_W, kr * kc), axis=-1)
        p = p.reshape(B, B_HEADS, GRID_W, kr, kc).astype(v_nb.dtype)
        return jnp.einsum('bhwrc,brwchd->bwhd', p, v_nb)

    out = lax.map(one_row, (jnp.arange(rows), qg))
    return out.transpose(1, 0, 2, 3, 4).reshape(B, S, B_HEADS * HEAD_DIM)


def memory_attn(q, mk, mv):
    B, S = q.shape[0], q.shape[1]
    s = jnp.einsum('bshd,bmhd->bhsm', q, mk, preferred_element_type=jnp.float32) * (HEAD_DIM ** -0.5)
    p = jax.nn.softmax(s, axis=-1).astype(mv.dtype)
    return jnp.einsum('bhsm,bmhd->bshd', p, mv).reshape(B, S, MEM_HEADS * HEAD_DIM)


def setup_inputs(seed: int = 0) -> dict:
    key = jax.random.key(seed)
    ks = jax.random.split(key, 16)
    n_a = (DEPTH + N_MIXERS - 1) // N_MIXERS
    n_b = (DEPTH + N_MIXERS - 2) // N_MIXERS

    def dense(k, shape, fan_in):
        return jax.random.normal(k, shape, jnp.float32) * fan_in ** -0.5

    def gain(k, shape):
        return 1.0 + 0.05 * jax.random.normal(k, shape, jnp.float32)

    return {
        'x': jax.random.normal(ks[0], (BATCH, SEQ, D_MODEL), jnp.float32),
        'mem': jax.random.normal(ks[1], (BATCH, MEM_LEN, D_MODEL), jnp.float32),
        'w_in_a': dense(ks[2], (n_a, D_MODEL, A_IN), D_MODEL),
        'sink_a': 0.5 * jax.random.normal(ks[3], (n_a, A_Q_HEADS), jnp.float32),
        'w_in_b': dense(ks[4], (n_b, D_MODEL, B_IN), D_MODEL),
        'rpb_b': 0.5 * jax.random.normal(ks[5], (n_b, B_HEADS, 2 * NB_ROWS_MAX - 1, 2 * NB_COLS - 1), jnp.float32),
        't5_table': 0.5 * jax.random.normal(ks[6], (N_BUCKETS, A_Q_HEADS), jnp.float32),
        'w_mem_kv': dense(ks[7], (DEPTH, D_MODEL, 2 * MEM_HEADS * HEAD_DIM), D_MODEL),
        'w_out': dense(ks[8], (DEPTH, MIX_WIDTH, D_MODEL), MIX_WIDTH),
        'w_gu': dense(ks[9], (DEPTH, D_MODEL, 2 * D_FF), D_MODEL),
        'w_down': dense(ks[10], (DEPTH, D_FF, D_MODEL), D_FF),
        'norm_mix_pre': gain(ks[11], (DEPTH, D_MODEL)),
        'norm_mix_post': gain(ks[12], (DEPTH, D_MODEL)),
        'norm_mem': gain(ks[13], (DEPTH, D_MODEL)),
        'norm_ffn_pre': gain(ks[14], (DEPTH, D_MODEL)),
        'norm_ffn_post': gain(ks[15], (DEPTH, D_MODEL)),
    }


def reference(x, mem, w_in_a, sink_a, w_in_b, rpb_b, t5_table, w_mem_kv, w_out, w_gu, w_down,
              norm_mix_pre, norm_mix_post, norm_mem, norm_ffn_pre, norm_ffn_post):
    B, S = x.shape[0], x.shape[1]
    hd = HEAD_DIM
    for i in range(DEPTH):
        h = rms_norm(x, norm_mix_pre[i])
        m = rms_norm(mem, norm_mem[i])
        mkv = jnp.einsum('bmd,de->bme', m, w_mem_kv[i])
        mk = mkv[..., :MEM_HEADS * hd].reshape(B, MEM_LEN, MEM_HEADS, hd)
        mv = mkv[..., MEM_HEADS * hd:].reshape(B, MEM_LEN, MEM_HEADS, hd)
        if i % N_MIXERS == 0:
            j = i // N_MIXERS
            proj = jnp.einsum('bsd,de->bse', h, w_in_a[j])
            c1 = A_Q_HEADS * hd
            c2 = c1 + A_KV_HEADS * hd
            c3 = c2 + A_KV_HEADS * hd
            q = proj[..., :c1].reshape(B, S, A_Q_HEADS, hd)
            k = proj[..., c1:c2].reshape(B, S, A_KV_HEADS, hd)
            v = proj[..., c2:c3].reshape(B, S, A_KV_HEADS, hd)
            qm = proj[..., c3:].reshape(B, S, MEM_HEADS, hd)
            tok = windowed_gqa(q, k, v, sink_a[j], t5_table)
        else:
            j = i // N_MIXERS
            proj = jnp.einsum('bsd,de->bse', h, w_in_b[j])
            c1 = B_HEADS * hd
            q = proj[..., :c1].reshape(B, S, B_HEADS, hd)
            k = proj[..., c1:2 * c1].reshape(B, S, B_HEADS, hd)
            v = proj[..., 2 * c1:3 * c1].reshape(B, S, B_HEADS, hd)
            qm = proj[..., 3 * c1:].reshape(B, S, MEM_HEADS, hd)
            tok = neighbourhood_attn(q, k, v, rpb_b[j])
        mem_out = memory_attn(qm, mk, mv)
        mixed = jnp.einsum('bse,ed->bsd', jnp.concatenate([tok, mem_out], axis=-1), w_out[i])
        x = x + rms_norm(mixed, norm_mix_post[i])
        h = rms_norm(x, norm_ffn_pre[i])
        gu = jnp.einsum('bsd,df->bsf', h, w_gu[i])
        f = jnp.einsum('bsf,fd->bsd', jax.nn.silu(gu[..., :D_FF]) * gu[..., D_FF:], w_down[i])
        x = x + rms_norm(f, norm_ffn_post[i])
    return x
```

```python
import functools
import math

import jax
import jax.numpy as jnp
import numpy as np
from jax import lax
from jax.experimental import pallas as pl
from jax.experimental.pallas import tpu as pltpu

F32 = jnp.float32
BF16 = jnp.bfloat16

D_MODEL = 1024
HEAD_DIM = 64
LANES = 128
A_Q_HEADS = 12
A_KV_HEADS = 4
A_GROUP = A_Q_HEADS // A_KV_HEADS
WINDOW = 128
BLOCK = 128
B_HEADS = 12
GRID_W = 64
NB_ROWS = 8
NB_COLS = 16
MEM_HEADS = 4
MEM_LEN = 256
N_BUCKETS = 32
MAX_DISTANCE = 128
D_FF = 2816
RMS_EPS = 1e-6
Q_SCALE = HEAD_DIM ** -0.5

A_GROUPS = (A_Q_HEADS + 2 * A_KV_HEADS + MEM_HEADS) * HEAD_DIM // LANES
B_GROUPS = (3 * B_HEADS + MEM_HEADS) * HEAD_DIM // LANES
CAT_GROUPS = (A_Q_HEADS + MEM_HEADS) * HEAD_DIM // LANES
TOK_GROUPS = A_Q_HEADS * HEAD_DIM // LANES
MEM_GROUPS = MEM_HEADS * HEAD_DIM // LANES

A_HEAD_ORDER = tuple(3 * (2 * p + hh) + t for p in range(2) for t in range(3) for hh in range(2))

TM_DENSE = 512
FF_CHUNK = 256
N_FF_CHUNKS = D_FF // FF_CHUNK
B_ROWS_PER_STEP = 4
B_TOK = B_ROWS_PER_STEP * GRID_W
B_KEY_ROWS = 3 * B_ROWS_PER_STEP
B_NK = 22
NEG_INF = float("-inf")
VMEM_LIMIT = 56 * 1024 * 1024


def _rms(x, g):
    y = x * lax.rsqrt(jnp.mean(x * x, axis=-1, keepdims=True) + RMS_EPS)
    return y * g


def _nt_dot(a, b):
    return lax.dot_general(a, b, (((1,), (1,)), ((), ())), preferred_element_type=F32)


def _const_spec(shape):
    n = len(shape)
    return pl.BlockSpec(shape, lambda *_: (0,) * n)


def _params(n_axes):
    return pltpu.CompilerParams(dimension_semantics=("arbitrary",) * n_axes, vmem_limit_bytes=VMEM_LIMIT)


def _memkv_kernel(mem_ref, g_ref, w_ref, o_ref):
    m = _rms(mem_ref[0], g_ref[0]).astype(BF16)
    res = jnp.dot(m, w_ref[0], preferred_element_type=F32)
    for g in range(2 * MEM_GROUPS):
        o_ref[0, 0, g] = res[:, g * LANES:(g + 1) * LANES].astype(BF16)


def _memkv(mem, norm_mem, w_mem_kv):
    depth, batch = w_mem_kv.shape[0], mem.shape[0]
    return pl.pallas_call(
        _memkv_kernel,
        out_shape=jax.ShapeDtypeStruct((depth, batch, 2 * MEM_GROUPS, MEM_LEN, LANES), BF16),
        grid=(depth, batch),
        in_specs=[
            pl.BlockSpec((1, MEM_LEN, D_MODEL), lambda l, b: (b, 0, 0)),
            pl.BlockSpec((1, 1, D_MODEL), lambda l, b: (l, 0, 0)),
            pl.BlockSpec((1, D_MODEL, 2 * MEM_GROUPS * LANES), lambda l, b: (l, 0, 0)),
        ],
        out_specs=pl.BlockSpec((1, 1, 2 * MEM_GROUPS, MEM_LEN, LANES), lambda l, b: (l, b, 0, 0, 0)),
        compiler_params=_params(2),
        name="mem_kv",
    )(mem, norm_mem.reshape(depth, 1, D_MODEL), w_mem_kv)


def _inproj_kernel(x_ref, g_ref, w_ref, o_ref, *, scaled_groups):
    h = _rms(x_ref[0], g_ref[...]).astype(BF16)
    res = jnp.dot(h, w_ref[...], preferred_element_type=F32)
    for g in range(o_ref.shape[1]):
        blk = res[:, g * LANES:(g + 1) * LANES]
        if g in scaled_groups:
            blk = blk * Q_SCALE
        o_ref[0, g] = blk.astype(BF16)


def _inproj(x, gain, w, scaled_groups):
    batch, seq, _ = x.shape
    n_groups = w.shape[1] // LANES
    tm = min(TM_DENSE, seq)
    return pl.pallas_call(
        functools.partial(_inproj_kernel, scaled_groups=scaled_groups),
        out_shape=jax.ShapeDtypeStruct((batch, n_groups, seq, LANES), BF16),
        grid=(batch, seq // tm),
        in_specs=[
            pl.BlockSpec((1, tm, D_MODEL), lambda b, i: (b, i, 0)),
            _const_spec((1, D_MODEL)),
            _const_spec(w.shape),
        ],
        out_specs=pl.BlockSpec((1, n_groups, tm, LANES), lambda b, i: (b, 0, i, 0)),
        compiler_params=_params(2),
        name="in_proj",
    )(x, gain.reshape(1, D_MODEL), w)


def _memory_heads(qm_ref, mk_ref, mv_ref, o_ref, lo_mask):
    tq = qm_ref.shape[2]
    for grp in range(MEM_GROUPS):
        qg = qm_ref[0, grp]
        zero = jnp.zeros_like(qg)
        lhs = jnp.concatenate([jnp.where(lo_mask, qg, zero), jnp.where(lo_mask, zero, qg)], axis=0)
        s = _nt_dot(lhs, mk_ref[0, grp])
        m = jnp.max(s, axis=-1, keepdims=True)
        p = jnp.exp(s - m)
        l = jnp.sum(p, axis=-1, keepdims=True)
        o = jnp.dot(p.astype(BF16), mv_ref[0, grp], preferred_element_type=F32) * (1.0 / l)
        o_ref[0, TOK_GROUPS + grp] = jnp.where(lo_mask, o[:tq], o[tq:]).astype(BF16)


def _t5_bucket(rel):
    half = N_BUCKETS // 2
    max_exact = half // 2
    n = -rel
    ret = jnp.where(n < 0, half, 0)
    n = jnp.abs(n)
    nf = jnp.maximum(n, 1).astype(jnp.float32)
    large = max_exact + (jnp.log(nf / max_exact) / math.log(MAX_DISTANCE / max_exact)
                         * (half - max_exact)).astype(jnp.int32)
    large = jnp.minimum(large, half - 1)
    return ret + jnp.where(n < max_exact, n, large)


def _bias_a_kernel(t5_ref, bucket_ref, o_ref):
    q = lax.broadcasted_iota(jnp.int32, (BLOCK, BLOCK), 0)
    j = lax.broadcasted_iota(jnp.int32, (BLOCK, BLOCK), 1)
    masked = jnp.full((BLOCK, BLOCK), NEG_INF, F32)
    valid = (j >= q, None, j <= q)
    for g in range(A_KV_HEADS):
        for t in range(A_GROUP):
            head = A_GROUP * g + t
            rows = slice(BLOCK * t, BLOCK * (t + 1))
            for kb, variant in ((0, 0), (1, 2), (2, 3)):
                bucket = bucket_ref[kb]
                acc = jnp.zeros((BLOCK, BLOCK), F32)
                for b in range(N_BUCKETS):
                    acc = jnp.where(bucket == b, t5_ref[b * A_Q_HEADS + head], acc)
                if valid[kb] is not None:
                    acc = jnp.where(valid[kb], acc, masked)
                o_ref[variant, g, rows, :] = acc
            o_ref[1, g, rows, :] = masked
            o_ref[4, g, rows, :] = masked


def _bias_a(t5_table):
    q = jnp.arange(BLOCK)[:, None]
    j = jnp.arange(BLOCK)[None, :]
    rel = jnp.stack([j - WINDOW - q, j - q, j + WINDOW - q])
    bucket = _t5_bucket(rel).astype(jnp.int32)
    return pl.pallas_call(
        _bias_a_kernel,
        out_shape=jax.ShapeDtypeStruct((5, A_KV_HEADS, A_GROUP * BLOCK, BLOCK), F32),
        in_specs=[pl.BlockSpec(memory_space=pltpu.SMEM), pl.BlockSpec(memory_space=pltpu.VMEM)],
        out_specs=pl.BlockSpec(memory_space=pltpu.VMEM),
        name="bias_a",
    )(t5_table.reshape(-1), bucket)


def _attn_a_kernel(q_ref, kp_ref, kc_ref, kn_ref, vp_ref, vc_ref, vn_ref, qm_ref, mk_ref, mv_ref, tb_ref,
                   sink_ref, o_ref):
    i = pl.program_id(1)
    last = pl.num_programs(1) - 1
    prev_v = jnp.where(i > 0, 0, 1)
    next_v = jnp.where(i < last, 3, 4)
    lo_mask = lax.broadcasted_iota(jnp.int32, (1, LANES), 1) < HEAD_DIM
    for p in range(A_KV_HEADS // 2):
        lhs3 = jnp.concatenate([q_ref[0, A_GROUP * p + t] for t in range(A_GROUP)], axis=0)
        zero = jnp.zeros_like(lhs3)
        kw = jnp.concatenate([kp_ref[0, p], kc_ref[0, p], kn_ref[0, p]], axis=0)
        vw = jnp.concatenate([vp_ref[0, p], vc_ref[0, p], vn_ref[0, p]], axis=0)
        outs = []
        for hh in range(2):
            g = 2 * p + hh
            lhs = jnp.where(lo_mask, lhs3, zero) if hh == 0 else jnp.where(lo_mask, zero, lhs3)
            s = _nt_dot(lhs, kw)
            bias = jnp.concatenate([tb_ref[prev_v, g], tb_ref[2, g], tb_ref[next_v, g]], axis=1)
            s = s + bias
            sink = sink_ref[g][:, 0:1]
            m = jnp.maximum(jnp.max(s, axis=-1, keepdims=True), sink)
            e = jnp.exp(s - m)
            l = jnp.sum(e, axis=-1, keepdims=True) + jnp.exp(sink - m)
            o = jnp.dot(e.astype(BF16), vw, preferred_element_type=F32)
            outs.append(o * (1.0 / l))
        o3 = jnp.where(lo_mask, outs[0], outs[1])
        for t in range(A_GROUP):
            o_ref[0, A_GROUP * p + t] = o3[BLOCK * t:BLOCK * (t + 1)].astype(BF16)
    _memory_heads(qm_ref, mk_ref, mv_ref, o_ref, lo_mask)


def _attn_a(proj, memkv, tb, sink_tab):
    batch, _, seq, _ = proj.shape
    nblk = seq // BLOCK
    assert nblk >= 2
    kv_blk = (1, 2, BLOCK, LANES)

    def halo(group_block, shift):
        return pl.BlockSpec(kv_blk, lambda b, i: (b, group_block, jnp.clip(i + shift, 0, nblk - 1), 0))

    return pl.pallas_call(
        _attn_a_kernel,
        out_shape=jax.ShapeDtypeStruct((batch, CAT_GROUPS, seq, LANES), BF16),
        grid=(batch, nblk),
        in_specs=[
            pl.BlockSpec((1, TOK_GROUPS, BLOCK, LANES), lambda b, i: (b, 0, i, 0)),
            halo(3, -1), halo(3, 0), halo(3, 1),
            halo(4, -1), halo(4, 0), halo(4, 1),
            pl.BlockSpec((1, MEM_GROUPS, BLOCK, LANES), lambda b, i: (b, 5, i, 0)),
            pl.BlockSpec((1, MEM_GROUPS, MEM_LEN, LANES), lambda b, i: (b, 0, 0, 0)),
            pl.BlockSpec((1, MEM_GROUPS, MEM_LEN, LANES), lambda b, i: (b, 1, 0, 0)),
            _const_spec(tb.shape),
            _const_spec(sink_tab.shape),
        ],
        out_specs=pl.BlockSpec((1, CAT_GROUPS, BLOCK, LANES), lambda b, i: (b, 0, i, 0)),
        compiler_params=_params(2),
        name="attn_a",
    )(proj, proj, proj, proj, proj, proj, proj, proj, memkv, memkv, tb, sink_tab)


def _bias_b_kernel(rpb_ref, o_ref):
    h = pl.program_id(0)
    n_ri = 2 * NB_ROWS - 1
    n_ci = 2 * NB_COLS - 1
    c = lax.broadcasted_iota(jnp.int32, (GRID_W, LANES), 0)
    lane = lax.broadcasted_iota(jnp.int32, (GRID_W, LANES), 1)
    cc = jnp.bitwise_and(lane, GRID_W - 1)
    hi_half = lane >= GRID_W
    cs = jnp.clip(c - NB_COLS // 2, 0, GRID_W - NB_COLS)
    col_valid = (cc >= cs) & (cc < cs + NB_COLS)
    dd = cc - c + NB_COLS - 1

    def body(k, carry):
        ri_l = k - 4
        ri_r = k - 3
        ok_l = (ri_l >= 0) & (ri_l < n_ri)
        ok_r = (ri_r >= 0) & (ri_r < n_ri)
        base_l = (h * n_ri + jnp.clip(ri_l, 0, n_ri - 1)) * n_ci
        base_r = (h * n_ri + jnp.clip(ri_r, 0, n_ri - 1)) * n_ci
        acc = jnp.zeros((GRID_W, LANES), F32)
        for d in range(n_ci):
            v_l = jnp.where(ok_l, rpb_ref[base_l + d], 0.0)
            v_r = jnp.where(ok_r, rpb_ref[base_r + d], 0.0)
            acc = jnp.where(dd == d, jnp.where(hi_half, v_r, v_l), acc)
        o_ref[0, k] = jnp.where(col_valid, acc, NEG_INF)
        return carry

    lax.fori_loop(0, B_NK, body, 0)


def _bias_b(rpb):
    return pl.pallas_call(
        _bias_b_kernel,
        out_shape=jax.ShapeDtypeStruct((B_HEADS, B_NK, GRID_W, LANES), F32),
        grid=(B_HEADS,),
        in_specs=[pl.BlockSpec(memory_space=pltpu.SMEM)],
        out_specs=pl.BlockSpec((1, B_NK, GRID_W, LANES), lambda h: (h, 0, 0, 0)),
        compiler_params=_params(1),
        name="bias_b",
    )(rpb.reshape(-1))


def _attn_b_kernel(q_ref, kp_ref, kc_ref, kn_ref, vp_ref, vc_ref, vn_ref, qm_ref, mk_ref, mv_ref, pv_ref, o_ref, *,
                   rows):
    i = pl.program_id(1)
    n_steps = pl.num_programs(1)
    ws = B_ROWS_PER_STEP * jnp.clip(i - 1, 0, n_steps - 3)
    lo_mask = lax.broadcasted_iota(jnp.int32, (1, LANES), 1) < HEAD_DIM
    n_keys = B_KEY_ROWS * GRID_W
    key_row = lax.broadcasted_iota(jnp.int32, (1, n_keys), 1) // GRID_W
    table_k0 = []
    row_pen = []
    for rl in range(B_ROWS_PER_STEP):
        r = B_ROWS_PER_STEP * i + rl
        rs = jnp.clip(r - NB_ROWS // 2, 0, rows - NB_ROWS)
        kr0 = rs - ws
        table_k0.append(rs - r - kr0 + NB_ROWS - 1 + 4)
        row_pen.append(jnp.where((key_row >= kr0) & (key_row < kr0 + NB_ROWS), 0.0, NEG_INF).astype(F32))
    for grp in range(TOK_GROUPS):
        qg = q_ref[0, grp]
        zero = jnp.zeros_like(qg)
        lhs = jnp.concatenate([jnp.where(lo_mask, qg, zero), jnp.where(lo_mask, zero, qg)], axis=0)
        kw = jnp.concatenate([kp_ref[0, grp], kc_ref[0, grp], kn_ref[0, grp]], axis=0)
        vw = jnp.concatenate([vp_ref[0, grp], vc_ref[0, grp], vn_ref[0, grp]], axis=0)
        s = _nt_dot(lhs, kw)
        biases = []
        for hh in range(2):
            head = 2 * grp + hh
            for rl in range(B_ROWS_PER_STEP):
                b = jnp.concatenate([pv_ref[head, table_k0[rl] + 2 * t] for t in range(B_KEY_ROWS // 2)], axis=1)
                biases.append(b + row_pen[rl])
        s = s + jnp.concatenate(biases, axis=0)
        m = jnp.max(s, axis=-1, keepdims=True)
        e = jnp.exp(s - m)
        l = jnp.sum(e, axis=-1, keepdims=True)
        o = jnp.dot(e.astype(BF16), vw, preferred_element_type=F32) * (1.0 / l)
        o_ref[0, grp] = jnp.where(lo_mask, o[:B_TOK], o[B_TOK:]).astype(BF16)
    _memory_heads(qm_ref, mk_ref, mv_ref, o_ref, lo_mask)


def _attn_b(proj, memkv, pv):
    batch, _, seq, _ = proj.shape
    rows = seq // GRID_W
    n_steps = rows // B_ROWS_PER_STEP
    assert n_steps >= 3 and rows >= NB_ROWS
    kv_blk = (1, TOK_GROUPS, B_TOK, LANES)

    def halo(group_block, pos):
        return pl.BlockSpec(kv_blk, lambda b, i: (b, group_block, jnp.clip(i - 1, 0, n_steps - 3) + pos, 0))

    return pl.pallas_call(
        functools.partial(_attn_b_kernel, rows=rows),
        out_shape=jax.ShapeDtypeStruct((batch, CAT_GROUPS, seq, LANES), BF16),
        grid=(batch, n_steps),
        in_specs=[
            pl.BlockSpec((1, TOK_GROUPS, B_TOK, LANES), lambda b, i: (b, 0, i, 0)),
            halo(1, 0), halo(1, 1), halo(1, 2),
            halo(2, 0), halo(2, 1), halo(2, 2),
            pl.BlockSpec((1, MEM_GROUPS, B_TOK, LANES), lambda b, i: (b, 3 * TOK_GROUPS // MEM_GROUPS, i, 0)),
            pl.BlockSpec((1, MEM_GROUPS, MEM_LEN, LANES), lambda b, i: (b, 0, 0, 0)),
            pl.BlockSpec((1, MEM_GROUPS, MEM_LEN, LANES), lambda b, i: (b, 1, 0, 0)),
            _const_spec(pv.shape),
        ],
        out_specs=pl.BlockSpec((1, CAT_GROUPS, B_TOK, LANES), lambda b, i: (b, 0, i, 0)),
        compiler_params=_params(2),
        name="attn_b",
    )(proj, proj, proj, proj, proj, proj, proj, proj, memkv, memkv, pv)


def _post_kernel(x_ref, cat_ref, wo_ref, wgu_ref, wd_ref, gpost_ref, gpre_ref, gfpost_ref, o_ref):
    cat = jnp.concatenate([cat_ref[0, g] for g in range(CAT_GROUPS)], axis=1)
    mixed = jnp.dot(cat, wo_ref[...], preferred_element_type=F32)
    x1 = x_ref[0] + _rms(mixed, gpost_ref[...])
    h2 = _rms(x1, gpre_ref[...]).astype(BF16)
    acc = jnp.zeros(x1.shape, F32)
    for c in range(N_FF_CHUNKS):
        gu = jnp.dot(h2, wgu_ref[c], preferred_element_type=F32)
        gate = gu[:, :FF_CHUNK]
        up = gu[:, FF_CHUNK:]
        act = (gate * jax.nn.sigmoid(gate) * up).astype(BF16)
        acc = acc + jnp.dot(act, wd_ref[c], preferred_element_type=F32)
    o_ref[0] = x1 + _rms(acc, gfpost_ref[...])


def _post(x, cat, w_out, w_gu, w_down, g_post, g_pre, g_fpost):
    batch, seq, _ = x.shape
    tm = min(TM_DENSE, seq)
    gain = lambda g: g.reshape(1, D_MODEL)
    return pl.pallas_call(
        _post_kernel,
        out_shape=jax.ShapeDtypeStruct(x.shape, F32),
        grid=(batch, seq // tm),
        in_specs=[
            pl.BlockSpec((1, tm, D_MODEL), lambda b, i: (b, i, 0)),
            pl.BlockSpec((1, CAT_GROUPS, tm, LANES), lambda b, i: (b, 0, i, 0)),
            _const_spec(w_out.shape),
            _const_spec(w_gu.shape),
            _const_spec(w_down.shape),
            _const_spec((1, D_MODEL)),
            _const_spec((1, D_MODEL)),
            _const_spec((1, D_MODEL)),
        ],
        out_specs=pl.BlockSpec((1, tm, D_MODEL), lambda b, i: (b, i, 0)),
        compiler_params=_params(2),
        name="post_ffn",
    )(x, cat, w_out, w_gu, w_down, gain(g_post), gain(g_pre), gain(g_fpost))


def _head_cols(order):
    return np.concatenate([np.arange(HEAD_DIM) + HEAD_DIM * h for h in order])


def _prep_w_in_a(w):
    q_cols = _head_cols(A_HEAD_ORDER)
    cols = np.concatenate([q_cols, np.arange(A_Q_HEADS * HEAD_DIM, w.shape[1])])
    return w[:, cols].astype(BF16)


def _prep_w_out_a(w):
    rows = np.concatenate([_head_cols(A_HEAD_ORDER), np.arange(A_Q_HEADS * HEAD_DIM, w.shape[0])])
    return w[rows, :].astype(BF16)


def _prep_w_gu(w):
    gate = w[:, :D_FF].reshape(D_MODEL, N_FF_CHUNKS, FF_CHUNK)
    up = w[:, D_FF:].reshape(D_MODEL, N_FF_CHUNKS, FF_CHUNK)
    return jnp.concatenate([gate, up], axis=2).transpose(1, 0, 2).astype(BF16)


def kernel(x, mem, w_in_a, sink_a, w_in_b, rpb_b, t5_table, w_mem_kv, w_out, w_gu, w_down, norm_mix_pre,
           norm_mix_post, norm_mem, norm_ffn_pre, norm_ffn_post):
    depth = w_out.shape[0]
    memkv = _memkv(mem, norm_mem, w_mem_kv.astype(BF16))
    tb = _bias_a(t5_table)
    a_scaled = frozenset(range(TOK_GROUPS)) | frozenset(range(A_GROUPS - MEM_GROUPS, A_GROUPS))
    b_scaled = frozenset(range(TOK_GROUPS)) | frozenset(range(B_GROUPS - MEM_GROUPS, B_GROUPS))
    for i in range(depth):
        j = i // 2
        if i % 2 == 0:
            proj = _inproj(x, norm_mix_pre[i], _prep_w_in_a(w_in_a[j]), a_scaled)
            sink_tab = jnp.broadcast_to(sink_a[j].astype(F32).reshape(A_KV_HEADS, A_GROUP, 1, 1),
                                        (A_KV_HEADS, A_GROUP, BLOCK, LANES)).reshape(A_KV_HEADS, A_GROUP * BLOCK, LANES)
            cat = _attn_a(proj, memkv[i], tb, sink_tab)
            wo = _prep_w_out_a(w_out[i])
        else:
            proj = _inproj(x, norm_mix_pre[i], w_in_b[j].astype(BF16), b_scaled)
            cat = _attn_b(proj, memkv[i], _bias_b(rpb_b[j]))
            wo = w_out[i].astype(BF16)
        x = _post(x, cat, wo, _prep_w_gu(w_gu[i]), w_down[i].reshape(N_FF_CHUNKS, FF_CHUNK, D_MODEL).astype(BF16),
                  norm_mix_post[i], norm_ffn_pre[i], norm_ffn_post[i])
    return x
```

```python
import functools
import math

import jax
import jax.numpy as jnp
import numpy as np
from jax import lax
from jax.experimental import pallas as pl
from jax.experimental.pallas import tpu as pltpu

F32 = jnp.float32
BF16 = jnp.bfloat16

D_MODEL = 1024
HEAD_DIM = 64
LANES = 128
A_Q_HEADS = 12
A_KV_HEADS = 4
A_GROUP = A_Q_HEADS // A_KV_HEADS
WINDOW = 128
BLOCK = 128
B_HEADS = 12
GRID_W = 64
NB_ROWS = 8
NB_COLS = 16
MEM_HEADS = 4
MEM_LEN = 256
N_BUCKETS = 32
MAX_DISTANCE = 128
D_FF = 2816
RMS_EPS = 1e-6
Q_SCALE = HEAD_DIM ** -0.5

A_GROUPS = (A_Q_HEADS + 2 * A_KV_HEADS + MEM_HEADS) * HEAD_DIM // LANES
B_GROUPS = (3 * B_HEADS + MEM_HEADS) * HEAD_DIM // LANES
CAT_GROUPS = (A_Q_HEADS + MEM_HEADS) * HEAD_DIM // LANES
TOK_GROUPS = A_Q_HEADS * HEAD_DIM // LANES
MEM_GROUPS = MEM_HEADS * HEAD_DIM // LANES

A_HEAD_ORDER = tuple(3 * (2 * p + hh) + t for p in range(2) for t in range(3) for hh in range(2))

TM_DENSE = 512
FF_CHUNK = 256
N_FF_CHUNKS = D_FF // FF_CHUNK
B_ROWS_PER_STEP = 4
B_TOK = B_ROWS_PER_STEP * GRID_W
B_KEY_ROWS = 3 * B_ROWS_PER_STEP
B_NK = 22
NEG_INF = float("-inf")
VMEM_LIMIT = 56 * 1024 * 1024


def _rms(x, g):
    y = x * lax.rsqrt(jnp.mean(x * x, axis=-1, keepdims=True) + RMS_EPS)
    return y * g


def _nt_dot(a, b):
    return lax.dot_general(a, b, (((1,), (1,)), ((), ())), preferred_element_type=F32)


def _const_spec(shape):
    n = len(shape)
    return pl.BlockSpec(shape, lambda *_: (0,) * n)


def _params(n_axes, flags=None):
    return pltpu.CompilerParams(dimension_semantics=("arbitrary",) * n_axes, vmem_limit_bytes=VMEM_LIMIT,
                                flags=flags)


def _memkv_kernel(mem_ref, g_ref, w_ref, o_ref):
    m = _rms(mem_ref[0], g_ref[0]).astype(BF16)
    res = jnp.dot(m, w_ref[0], preferred_element_type=F32)
    for g in range(2 * MEM_GROUPS):
        o_ref[0, 0, g] = res[:, g * LANES:(g + 1) * LANES].astype(BF16)


def _memkv(mem, norm_mem, w_mem_kv):
    depth, batch = w_mem_kv.shape[0], mem.shape[0]
    return pl.pallas_call(
        _memkv_kernel,
        out_shape=jax.ShapeDtypeStruct((depth, batch, 2 * MEM_GROUPS, MEM_LEN, LANES), BF16),
        grid=(depth, batch),
        in_specs=[
            pl.BlockSpec((1, MEM_LEN, D_MODEL), lambda l, b: (b, 0, 0)),
            pl.BlockSpec((1, 1, D_MODEL), lambda l, b: (l, 0, 0)),
            pl.BlockSpec((1, D_MODEL, 2 * MEM_GROUPS * LANES), lambda l, b: (l, 0, 0)),
        ],
        out_specs=pl.BlockSpec((1, 1, 2 * MEM_GROUPS, MEM_LEN, LANES), lambda l, b: (l, b, 0, 0, 0)),
        compiler_params=_params(2),
        name="mem_kv",
    )(mem, norm_mem.reshape(depth, 1, D_MODEL), w_mem_kv)


def _inproj_kernel(x_ref, g_ref, w_ref, o_ref, *, scaled_groups):
    h = _rms(x_ref[0], g_ref[...]).astype(BF16)
    res = jnp.dot(h, w_ref[...], preferred_element_type=F32)
    for g in range(o_ref.shape[1]):
        blk = res[:, g * LANES:(g + 1) * LANES]
        if g in scaled_groups:
            blk = blk * Q_SCALE
        o_ref[0, g] = blk.astype(BF16)


def _inproj(x, gain, w, scaled_groups):
    batch, seq, _ = x.shape
    n_groups = w.shape[1] // LANES
    tm = min(TM_DENSE, seq)
    return pl.pallas_call(
        functools.partial(_inproj_kernel, scaled_groups=scaled_groups),
        out_shape=jax.ShapeDtypeStruct((batch, n_groups, seq, LANES), BF16),
        grid=(batch, seq // tm),
        in_specs=[
            pl.BlockSpec((1, tm, D_MODEL), lambda b, i: (b, i, 0)),
            _const_spec((1, D_MODEL)),
            _const_spec(w.shape),
        ],
        out_specs=pl.BlockSpec((1, n_groups, tm, LANES), lambda b, i: (b, 0, i, 0)),
        compiler_params=_params(2),
        name="in_proj",
    )(x, gain.reshape(1, D_MODEL), w)


def _split_halves(x, lo_mask):
    zero = jnp.zeros_like(x)
    return jnp.concatenate([jnp.where(lo_mask, x, zero), jnp.where(lo_mask, zero, x)], axis=0)


def _pv_with_rowsum(e_bf16, v):
    v_aug = jnp.concatenate([v, jnp.ones_like(v)], axis=1)
    o_aug = jnp.dot(e_bf16, v_aug, preferred_element_type=F32)
    return o_aug[:, :LANES], o_aug[:, LANES:]


def _memory_heads(qm_ref, mk_ref, mv_ref, o_ref, lo_mask):
    tq = qm_ref.shape[2]
    for grp in range(MEM_GROUPS):
        qg = qm_ref[0, grp]
        zero = jnp.zeros_like(qg)
        lhs = jnp.concatenate([jnp.where(lo_mask, qg, zero), jnp.where(lo_mask, zero, qg)], axis=0)
        s = _nt_dot(lhs, mk_ref[0, grp])
        m = jnp.max(s, axis=-1, keepdims=True)
        e = jnp.exp(s - m).astype(BF16)
        o, l = _pv_with_rowsum(e, mv_ref[0, grp])
        o = o / l
        o_ref[0, TOK_GROUPS + grp] = jnp.where(lo_mask, o[:tq], o[tq:]).astype(BF16)


def _t5_bucket(rel):
    half = N_BUCKETS // 2
    max_exact = half // 2
    n = -rel
    ret = jnp.where(n < 0, half, 0)
    n = jnp.abs(n)
    nf = jnp.maximum(n, 1).astype(jnp.float32)
    large = max_exact + (jnp.log(nf / max_exact) / math.log(MAX_DISTANCE / max_exact)
                         * (half - max_exact)).astype(jnp.int32)
    large = jnp.minimum(large, half - 1)
    return ret + jnp.where(n < max_exact, n, large)


def _bias_a_kernel(t5_ref, bucket_ref, o_ref):
    q = lax.broadcasted_iota(jnp.int32, (BLOCK, BLOCK), 0)
    j = lax.broadcasted_iota(jnp.int32, (BLOCK, BLOCK), 1)
    masked = jnp.full((BLOCK, BLOCK), NEG_INF, F32)
    valid = (j >= q, None, j <= q)
    for g in range(A_KV_HEADS):
        for t in range(A_GROUP):
            head = A_GROUP * g + t
            rows = slice(BLOCK * t, BLOCK * (t + 1))
            for kb, variant in ((0, 0), (1, 2), (2, 3)):
                bucket = bucket_ref[kb]
                acc = jnp.zeros((BLOCK, BLOCK), F32)
                for b in range(N_BUCKETS):
                    acc = jnp.where(bucket == b, t5_ref[b * A_Q_HEADS + head], acc)
                if valid[kb] is not None:
                    acc = jnp.where(valid[kb], acc, masked)
                o_ref[variant, g, rows, :] = acc
            o_ref[1, g, rows, :] = masked
            o_ref[4, g, rows, :] = masked


def _bias_a(t5_table):
    q = jnp.arange(BLOCK)[:, None]
    j = jnp.arange(BLOCK)[None, :]
    rel = jnp.stack([j - WINDOW - q, j - q, j + WINDOW - q])
    bucket = _t5_bucket(rel).astype(jnp.int32)
    return pl.pallas_call(
        _bias_a_kernel,
        out_shape=jax.ShapeDtypeStruct((5, A_KV_HEADS, A_GROUP * BLOCK, BLOCK), F32),
        in_specs=[pl.BlockSpec(memory_space=pltpu.SMEM), pl.BlockSpec(memory_space=pltpu.VMEM)],
        out_specs=pl.BlockSpec(memory_space=pltpu.VMEM),
        name="bias_a",
    )(t5_table.reshape(-1), bucket)


def _attn_a_kernel(q_ref, kp_ref, kc_ref, kn_ref, vp_ref, vc_ref, vn_ref, qm_ref, mk_ref, mv_ref, tb_ref,
                   sink_ref, o_ref):
    i = pl.program_id(1)
    last = pl.num_programs(1) - 1
    prev_v = jnp.where(i > 0, 0, 1)
    next_v = jnp.where(i < last, 3, 4)
    lo_mask = lax.broadcasted_iota(jnp.int32, (1, LANES), 1) < HEAD_DIM
    variants = (prev_v, 2, next_v)
    n_keys = 3 * BLOCK
    scores = []
    for p in range(A_KV_HEADS // 2):
        lhs3 = jnp.concatenate([q_ref[0, A_GROUP * p + t] for t in range(A_GROUP)], axis=0)
        kw = jnp.concatenate([kp_ref[0, p], kc_ref[0, p], kn_ref[0, p]], axis=0)
        scores.append(_nt_dot(lhs3, _split_halves(kw, lo_mask)))
    for p in range(A_KV_HEADS // 2):
        vw = jnp.concatenate([vp_ref[0, p], vc_ref[0, p], vn_ref[0, p]], axis=0)
        es, sink_terms = [], []
        for hh in range(2):
            g = 2 * p + hh
            s = [scores[p][:, n_keys * hh + LANES * j:n_keys * hh + LANES * (j + 1)] + tb_ref[variants[j], g]
                 for j in range(3)]
            sink = sink_ref[g]
            m = jnp.maximum(jnp.max(jnp.maximum(jnp.maximum(s[0], s[1]), s[2]), axis=-1, keepdims=True), sink)
            es += [jnp.exp(sj - m).astype(BF16) for sj in s]
            sink_terms.append(jnp.exp(sink - m))
        lo_ones = jnp.where(lax.broadcasted_iota(jnp.int32, vw.shape, 1) < HEAD_DIM, 1.0, 0.0)
        ones_halves = jnp.concatenate([lo_ones, 1.0 - lo_ones], axis=0).astype(BF16)
        v_aug = jnp.concatenate([_split_halves(vw, lo_mask), ones_halves], axis=1)
        o_aug = jnp.dot(jnp.concatenate(es, axis=1), v_aug, preferred_element_type=F32)
        lo_f32 = lax.broadcasted_iota(jnp.int32, sink_terms[0].shape, 1) < HEAD_DIM
        l = o_aug[:, LANES:] + jnp.where(lo_f32, sink_terms[0], sink_terms[1])
        o3 = o_aug[:, :LANES] / l
        for t in range(A_GROUP):
            o_ref[0, A_GROUP * p + t] = o3[BLOCK * t:BLOCK * (t + 1)].astype(BF16)
    _memory_heads(qm_ref, mk_ref, mv_ref, o_ref, lo_mask)


def _attn_a(proj, memkv, tb, sink_tab):
    batch, _, seq, _ = proj.shape
    nblk = seq // BLOCK
    assert nblk >= 2
    kv_blk = (1, 2, BLOCK, LANES)

    def halo(group_block, shift):
        return pl.BlockSpec(kv_blk, lambda b, i: (b, group_block, jnp.clip(i + shift, 0, nblk - 1), 0))

    return pl.pallas_call(
        _attn_a_kernel,
        out_shape=jax.ShapeDtypeStruct((batch, CAT_GROUPS, seq, LANES), BF16),
        grid=(batch, nblk),
        in_specs=[
            pl.BlockSpec((1, TOK_GROUPS, BLOCK, LANES), lambda b, i: (b, 0, i, 0)),
            halo(3, -1), halo(3, 0), halo(3, 1),
            halo(4, -1), halo(4, 0), halo(4, 1),
            pl.BlockSpec((1, MEM_GROUPS, BLOCK, LANES), lambda b, i: (b, 5, i, 0)),
            pl.BlockSpec((1, MEM_GROUPS, MEM_LEN, LANES), lambda b, i: (b, 0, 0, 0)),
            pl.BlockSpec((1, MEM_GROUPS, MEM_LEN, LANES), lambda b, i: (b, 1, 0, 0)),
            _const_spec(tb.shape),
            _const_spec(sink_tab.shape),
        ],
        out_specs=pl.BlockSpec((1, CAT_GROUPS, BLOCK, LANES), lambda b, i: (b, 0, i, 0)),
        compiler_params=_params(2),
        name="attn_a",
    )(proj, proj, proj, proj, proj, proj, proj, proj, memkv, memkv, tb, sink_tab)


def _bias_b_kernel(rpb_ref, o_ref):
    h = pl.program_id(0)
    n_ri = 2 * NB_ROWS - 1
    n_ci = 2 * NB_COLS - 1
    c = lax.broadcasted_iota(jnp.int32, (GRID_W, LANES), 0)
    lane = lax.broadcasted_iota(jnp.int32, (GRID_W, LANES), 1)
    cc = jnp.bitwise_and(lane, GRID_W - 1)
    hi_half = lane >= GRID_W
    cs = jnp.clip(c - NB_COLS // 2, 0, GRID_W - NB_COLS)
    col_valid = (cc >= cs) & (cc < cs + NB_COLS)
    dd = cc - c + NB_COLS - 1

    def body(k, carry):
        ri_l = k - 4
        ri_r = k - 3
        ok_l = (ri_l >= 0) & (ri_l < n_ri)
        ok_r = (ri_r >= 0) & (ri_r < n_ri)
        base_l = (h * n_ri + jnp.clip(ri_l, 0, n_ri - 1)) * n_ci
        base_r = (h * n_ri + jnp.clip(ri_r, 0, n_ri - 1)) * n_ci
        acc = jnp.zeros((GRID_W, LANES), F32)
        for d in range(n_ci):
            v_l = jnp.where(ok_l, rpb_ref[base_l + d], 0.0)
            v_r = jnp.where(ok_r, rpb_ref[base_r + d], 0.0)
            acc = jnp.where(dd == d, jnp.where(hi_half, v_r, v_l), acc)
        o_ref[0, k] = jnp.where(col_valid, acc, NEG_INF)
        return carry

    lax.fori_loop(0, B_NK, body, 0)


def _bias_b(rpb):
    return pl.pallas_call(
        _bias_b_kernel,
        out_shape=jax.ShapeDtypeStruct((B_HEADS, B_NK, GRID_W, LANES), F32),
        grid=(B_HEADS,),
        in_specs=[pl.BlockSpec(memory_space=pltpu.SMEM)],
        out_specs=pl.BlockSpec((1, B_NK, GRID_W, LANES), lambda h: (h, 0, 0, 0)),
        compiler_params=_params(1),
        name="bias_b",
    )(rpb.reshape(-1))


def _attn_b_kernel(q_ref, kp_ref, kc_ref, kn_ref, vp_ref, vc_ref, vn_ref, qm_ref, mk_ref, mv_ref, pv_ref, o_ref, *,
                   rows):
    i = pl.program_id(1)
    n_steps = pl.num_programs(1)
    ws = B_ROWS_PER_STEP * jnp.clip(i - 1, 0, n_steps - 3)
    lo_mask = lax.broadcasted_iota(jnp.int32, (1, LANES), 1) < HEAD_DIM
    n_keys = B_KEY_ROWS * GRID_W
    key_row = lax.broadcasted_iota(jnp.int32, (1, n_keys), 1) // GRID_W
    table_k0 = []
    row_pen = []
    for rl in range(B_ROWS_PER_STEP):
        r = B_ROWS_PER_STEP * i + rl
        rs = jnp.clip(r - NB_ROWS // 2, 0, rows - NB_ROWS)
        kr0 = rs - ws
        table_k0.append(rs - r - kr0 + NB_ROWS - 1 + 4)
        row_pen.append(jnp.where((key_row >= kr0) & (key_row < kr0 + NB_ROWS), 0.0, NEG_INF).astype(F32))
    for grp in range(TOK_GROUPS):
        qg = q_ref[0, grp]
        zero = jnp.zeros_like(qg)
        lhs = jnp.concatenate([jnp.where(lo_mask, qg, zero), jnp.where(lo_mask, zero, qg)], axis=0)
        kw = jnp.concatenate([kp_ref[0, grp], kc_ref[0, grp], kn_ref[0, grp]], axis=0)
        vw = jnp.concatenate([vp_ref[0, grp], vc_ref[0, grp], vn_ref[0, grp]], axis=0)
        s = _nt_dot(lhs, kw)
        biases = []
        for hh in range(2):
            head = 2 * grp + hh
            for rl in range(B_ROWS_PER_STEP):
                b = jnp.concatenate([pv_ref[head, table_k0[rl] + 2 * t] for t in range(B_KEY_ROWS // 2)], axis=1)
                biases.append(b + row_pen[rl])
        s = s + jnp.concatenate(biases, axis=0)
        m = jnp.max(s, axis=-1, keepdims=True)
        o, l = _pv_with_rowsum(jnp.exp(s - m).astype(BF16), vw)
        o = o / l
        o_ref[0, grp] = jnp.where(lo_mask, o[:B_TOK], o[B_TOK:]).astype(BF16)
    _memory_heads(qm_ref, mk_ref, mv_ref, o_ref, lo_mask)


def _attn_b(proj, memkv, pv):
    batch, _, seq, _ = proj.shape
    rows = seq // GRID_W
    n_steps = rows // B_ROWS_PER_STEP
    assert n_steps >= 3 and rows >= NB_ROWS
    kv_blk = (1, TOK_GROUPS, B_TOK, LANES)

    def halo(group_block, pos):
        return pl.BlockSpec(kv_blk, lambda b, i: (b, group_block, jnp.clip(i - 1, 0, n_steps - 3) + pos, 0))

    return pl.pallas_call(
        functools.partial(_attn_b_kernel, rows=rows),
        out_shape=jax.ShapeDtypeStruct((batch, CAT_GROUPS, seq, LANES), BF16),
        grid=(batch, n_steps),
        in_specs=[
            pl.BlockSpec((1, TOK_GROUPS, B_TOK, LANES), lambda b, i: (b, 0, i, 0)),
            halo(1, 0), halo(1, 1), halo(1, 2),
            halo(2, 0), halo(2, 1), halo(2, 2),
            pl.BlockSpec((1, MEM_GROUPS, B_TOK, LANES), lambda b, i: (b, 3 * TOK_GROUPS // MEM_GROUPS, i, 0)),
            pl.BlockSpec((1, MEM_GROUPS, MEM_LEN, LANES), lambda b, i: (b, 0, 0, 0)),
            pl.BlockSpec((1, MEM_GROUPS, MEM_LEN, LANES), lambda b, i: (b, 1, 0, 0)),
            _const_spec(pv.shape),
        ],
        out_specs=pl.BlockSpec((1, CAT_GROUPS, B_TOK, LANES), lambda b, i: (b, 0, i, 0)),
        compiler_params=_params(2),
        name="attn_b",
    )(proj, proj, proj, proj, proj, proj, proj, proj, memkv, memkv, pv)


def _post_kernel(x_ref, cat_ref, wo_ref, wgu_ref, wd_ref, gpost_ref, gpre_ref, gfpost_ref, o_ref):
    cat = jnp.concatenate([cat_ref[0, g] for g in range(CAT_GROUPS)], axis=1)
    mixed = jnp.dot(cat, wo_ref[...], preferred_element_type=F32)
    x1 = x_ref[0] + _rms(mixed, gpost_ref[...])
    h2 = _rms(x1, gpre_ref[...]).astype(BF16)
    acc = jnp.zeros(x1.shape, F32)
    for c in range(N_FF_CHUNKS):
        gu = jnp.dot(h2, wgu_ref[c], preferred_element_type=F32)
        gate = gu[:, :FF_CHUNK]
        up = gu[:, FF_CHUNK:]
        act = (gate * jax.nn.sigmoid(gate) * up).astype(BF16)
        acc = acc + jnp.dot(act, wd_ref[c], preferred_element_type=F32)
    o_ref[0] = x1 + _rms(acc, gfpost_ref[...])


def _post(x, cat, w_out, w_gu, w_down, g_post, g_pre, g_fpost):
    batch, seq, _ = x.shape
    tm = min(TM_DENSE, seq)
    gain = lambda g: g.reshape(1, D_MODEL)
    return pl.pallas_call(
        _post_kernel,
        out_shape=jax.ShapeDtypeStruct(x.shape, F32),
        grid=(batch, seq // tm),
        in_specs=[
            pl.BlockSpec((1, tm, D_MODEL), lambda b, i: (b, i, 0)),
            pl.BlockSpec((1, CAT_GROUPS, tm, LANES), lambda b, i: (b, 0, i, 0)),
            _const_spec(w_out.shape),
            _const_spec(w_gu.shape),
            _const_spec(w_down.shape),
            _const_spec((1, D_MODEL)),
            _const_spec((1, D_MODEL)),
            _const_spec((1, D_MODEL)),
        ],
        out_specs=pl.BlockSpec((1, tm, D_MODEL), lambda b, i: (b, i, 0)),
        compiler_params=_params(2),
        name="post_ffn",
    )(x, cat, w_out, w_gu, w_down, gain(g_post), gain(g_pre), gain(g_fpost))


def _head_cols(order):
    return np.concatenate([np.arange(HEAD_DIM) + HEAD_DIM * h for h in order])


def _prep_w_in_a(w):
    q_cols = _head_cols(A_HEAD_ORDER)
    cols = np.concatenate([q_cols, np.arange(A_Q_HEADS * HEAD_DIM, w.shape[1])])
    return w[:, cols].astype(BF16)


def _prep_w_out_a(w):
    rows = np.concatenate([_head_cols(A_HEAD_ORDER), np.arange(A_Q_HEADS * HEAD_DIM, w.shape[0])])
    return w[rows, :].astype(BF16)


def _prep_w_gu(w):
    gate = w[:, :D_FF].reshape(D_MODEL, N_FF_CHUNKS, FF_CHUNK)
    up = w[:, D_FF:].reshape(D_MODEL, N_FF_CHUNKS, FF_CHUNK)
    return jnp.concatenate([gate, up], axis=2).transpose(1, 0, 2).astype(BF16)


def kernel(x, mem, w_in_a, sink_a, w_in_b, rpb_b, t5_table, w_mem_kv, w_out, w_gu, w_down, norm_mix_pre,
           norm_mix_post, norm_mem, norm_ffn_pre, norm_ffn_post):
    depth = w_out.shape[0]
    memkv = _memkv(mem, norm_mem, w_mem_kv.astype(BF16))
    tb = _bias_a(t5_table)
    a_scaled = frozenset(range(TOK_GROUPS)) | frozenset(range(A_GROUPS - MEM_GROUPS, A_GROUPS))
    b_scaled = frozenset(range(TOK_GROUPS)) | frozenset(range(B_GROUPS - MEM_GROUPS, B_GROUPS))
    for i in range(depth):
        j = i // 2
        if i % 2 == 0:
            proj = _inproj(x, norm_mix_pre[i], _prep_w_in_a(w_in_a[j]), a_scaled)
            sink_tab = jnp.broadcast_to(sink_a[j].astype(F32).reshape(A_KV_HEADS, A_GROUP, 1, 1),
                                        (A_KV_HEADS, A_GROUP, BLOCK, LANES)).reshape(A_KV_HEADS, A_GROUP * BLOCK, LANES)
            cat = _attn_a(proj, memkv[i], tb, sink_tab)
            wo = _prep_w_out_a(w_out[i])
        else:
            proj = _inproj(x, norm_mix_pre[i], w_in_b[j].astype(BF16), b_scaled)
            cat = _attn_b(proj, memkv[i], _bias_b(rpb_b[j]))
            wo = w_out[i].astype(BF16)
        x = _post(x, cat, wo, _prep_w_gu(w_gu[i]), w_down[i].reshape(N_FF_CHUNKS, FF_CHUNK, D_MODEL).astype(BF16),
                  norm_mix_post[i], norm_ffn_pre[i], norm_ffn_post[i])
    return x
```

```python
import functools
import math

import jax
import jax.numpy as jnp
import numpy as np
from jax import lax
from jax.experimental import pallas as pl
from jax.experimental.pallas import tpu as pltpu

F32 = jnp.float32
BF16 = jnp.bfloat16

D_MODEL = 1024
HEAD_DIM = 64
LANES = 128
A_Q_HEADS = 12
A_KV_HEADS = 4
A_GROUP = A_Q_HEADS // A_KV_HEADS
WINDOW = 128
BLOCK = 128
B_HEADS = 12
GRID_W = 64
NB_ROWS = 8
NB_COLS = 16
MEM_HEADS = 4
MEM_LEN = 256
N_BUCKETS = 32
MAX_DISTANCE = 128
D_FF = 2816
RMS_EPS = 1e-6
Q_SCALE = HEAD_DIM ** -0.5

A_GROUPS = (A_Q_HEADS + 2 * A_KV_HEADS + MEM_HEADS) * HEAD_DIM // LANES
B_GROUPS = (3 * B_HEADS + MEM_HEADS) * HEAD_DIM // LANES
CAT_GROUPS = (A_Q_HEADS + MEM_HEADS) * HEAD_DIM // LANES
TOK_GROUPS = A_Q_HEADS * HEAD_DIM // LANES
MEM_GROUPS = MEM_HEADS * HEAD_DIM // LANES

A_HEAD_ORDER = tuple(3 * (2 * p + hh) + t for p in range(2) for t in range(3) for hh in range(2))

TM_DENSE = 1024
POST_CHAINS = 2
FF_CHUNKS = (256,) * 11
assert sum(FF_CHUNKS) == D_FF
A_TQ = 512
A_LOOKAHEAD = 2
B_ROWS_PER_STEP = 4
B_TOK = B_ROWS_PER_STEP * GRID_W
B_KEY_ROWS = 3 * B_ROWS_PER_STEP
B_NK = 22
NEG_INF = float("-inf")
VMEM_LIMIT = 56 * 1024 * 1024


def _rms(x, g):
    y = x * lax.rsqrt(jnp.mean(x * x, axis=-1, keepdims=True) + RMS_EPS)
    return y * g


def _nt_dot(a, b):
    return lax.dot_general(a, b, (((1,), (1,)), ((), ())), preferred_element_type=F32)


def _const_spec(shape):
    n = len(shape)
    return pl.BlockSpec(shape, lambda *_: (0,) * n, pipeline_mode=pl.Buffered(1))


def _params(n_axes, flags=None):
    return pltpu.CompilerParams(dimension_semantics=("arbitrary",) * n_axes, vmem_limit_bytes=VMEM_LIMIT,
                                flags=flags)


def _memkv_kernel(mem_ref, g_ref, w_ref, o_ref):
    m = _rms(mem_ref[0], g_ref[0]).astype(BF16)
    res = jnp.dot(m, w_ref[0], preferred_element_type=F32)
    for g in range(2 * MEM_GROUPS):
        o_ref[0, 0, g] = res[:, g * LANES:(g + 1) * LANES].astype(BF16)


def _memkv(mem, norm_mem, w_mem_kv):
    depth, batch = w_mem_kv.shape[0], mem.shape[0]
    return pl.pallas_call(
        _memkv_kernel,
        out_shape=jax.ShapeDtypeStruct((depth, batch, 2 * MEM_GROUPS, MEM_LEN, LANES), BF16),
        grid=(depth, batch),
        in_specs=[
            pl.BlockSpec((1, MEM_LEN, D_MODEL), lambda l, b: (b, 0, 0)),
            pl.BlockSpec((1, 1, D_MODEL), lambda l, b: (l, 0, 0)),
            pl.BlockSpec((1, D_MODEL, 2 * MEM_GROUPS * LANES), lambda l, b: (l, 0, 0)),
        ],
        out_specs=pl.BlockSpec((1, 1, 2 * MEM_GROUPS, MEM_LEN, LANES), lambda l, b: (l, b, 0, 0, 0)),
        compiler_params=_params(2),
        name="mem_kv",
    )(mem, norm_mem.reshape(depth, 1, D_MODEL), w_mem_kv)


def _store_groups(o_ref, rows, res, scaled_groups):
    for g in range(o_ref.shape[1]):
        blk = res[:, g * LANES:(g + 1) * LANES]
        if g in scaled_groups:
            blk = blk * Q_SCALE
        o_ref[0, g, rows, :] = blk.astype(BF16)


def _inproj_kernel(x_ref, g_ref, w_ref, o_ref, *, scaled_groups):
    h = _rms(x_ref[0], g_ref[...]).astype(BF16)
    res = jnp.dot(h, w_ref[...], preferred_element_type=F32)
    _store_groups(o_ref, slice(None), res, scaled_groups)


def _inproj(x, gain, w, scaled_groups):
    batch, seq, _ = x.shape
    n_groups = w.shape[1] // LANES
    tm = min(TM_DENSE, seq)
    return pl.pallas_call(
        functools.partial(_inproj_kernel, scaled_groups=scaled_groups),
        out_shape=jax.ShapeDtypeStruct((batch, n_groups, seq, LANES), BF16),
        grid=(batch, seq // tm),
        in_specs=[
            pl.BlockSpec((1, tm, D_MODEL), lambda b, i: (b, i, 0)),
            _const_spec((1, D_MODEL)),
            _const_spec(w.shape),
        ],
        out_specs=pl.BlockSpec((1, n_groups, tm, LANES), lambda b, i: (b, 0, i, 0)),
        compiler_params=_params(2),
        name="in_proj",
    )(x, gain.reshape(1, D_MODEL), w)


def _split_halves(x, lo_mask):
    zero = jnp.zeros_like(x)
    return jnp.concatenate([jnp.where(lo_mask, x, zero), jnp.where(lo_mask, zero, x)], axis=0)


def _pv_with_rowsum(e_bf16, v):
    v_aug = jnp.concatenate([v, jnp.ones_like(v)], axis=1)
    o_aug = jnp.dot(e_bf16, v_aug, preferred_element_type=F32)
    return o_aug[:, :LANES], o_aug[:, LANES:]


def _memory_heads(qm_ref, mk_ref, mv_ref, o_ref, lo_mask):
    tq = qm_ref.shape[2]
    for grp in range(MEM_GROUPS):
        qg = qm_ref[0, grp]
        zero = jnp.zeros_like(qg)
        lhs = jnp.concatenate([jnp.where(lo_mask, qg, zero), jnp.where(lo_mask, zero, qg)], axis=0)
        s = _nt_dot(lhs, mk_ref[0, grp])
        m = jnp.max(s, axis=-1, keepdims=True)
        e = jnp.exp(s - m).astype(BF16)
        o, l = _pv_with_rowsum(e, mv_ref[0, grp])
        o = o / l
        o_ref[0, TOK_GROUPS + grp] = jnp.where(lo_mask, o[:tq], o[tq:]).astype(BF16)


def _t5_bucket(rel):
    half = N_BUCKETS // 2
    max_exact = half // 2
    n = -rel
    ret = jnp.where(n < 0, half, 0)
    n = jnp.abs(n)
    nf = jnp.maximum(n, 1).astype(jnp.float32)
    large = max_exact + (jnp.log(nf / max_exact) / math.log(MAX_DISTANCE / max_exact)
                         * (half - max_exact)).astype(jnp.int32)
    large = jnp.minimum(large, half - 1)
    return ret + jnp.where(n < max_exact, n, large)


def _bias_a_kernel(t5_ref, bucket_ref, o_ref):
    q = lax.broadcasted_iota(jnp.int32, (BLOCK, BLOCK), 0)
    j = lax.broadcasted_iota(jnp.int32, (BLOCK, BLOCK), 1)
    masked = jnp.full((BLOCK, BLOCK), NEG_INF, F32)
    valid = (j >= q, None, j <= q)
    for g in range(A_KV_HEADS):
        for t in range(A_GROUP):
            head = A_GROUP * g + t
            rows = slice(BLOCK * t, BLOCK * (t + 1))
            for kb, variant in ((0, 0), (1, 2), (2, 3)):
                bucket = bucket_ref[kb]
                acc = jnp.zeros((BLOCK, BLOCK), F32)
                for b in range(N_BUCKETS):
                    acc = jnp.where(bucket == b, t5_ref[b * A_Q_HEADS + head], acc)
                if valid[kb] is not None:
                    acc = jnp.where(valid[kb], acc, masked)
                o_ref[variant, g, rows, :] = acc
            o_ref[1, g, rows, :] = masked
            o_ref[4, g, rows, :] = masked


def _bias_a(t5_table):
    q = jnp.arange(BLOCK)[:, None]
    j = jnp.arange(BLOCK)[None, :]
    rel = jnp.stack([j - WINDOW - q, j - q, j + WINDOW - q])
    bucket = _t5_bucket(rel).astype(jnp.int32)
    return pl.pallas_call(
        _bias_a_kernel,
        out_shape=jax.ShapeDtypeStruct((5, A_KV_HEADS, A_GROUP * BLOCK, BLOCK), F32),
        in_specs=[pl.BlockSpec(memory_space=pltpu.SMEM), pl.BlockSpec(memory_space=pltpu.VMEM)],
        out_specs=pl.BlockSpec(memory_space=pltpu.VMEM),
        name="bias_a",
    )(t5_table.reshape(-1), bucket)


def _attn_a_kernel(q_ref, kp_ref, kc_ref, kn_ref, vp_ref, vc_ref, vn_ref, qm_ref, mk_ref, mv_ref, tb_ref,
                   sink_ref, o_ref):
    n_sub = q_ref.shape[2] // BLOCK
    i = pl.program_id(1)
    last = pl.num_programs(1) - 1
    first_prev_v = jnp.where(i > 0, 0, 1)
    last_next_v = jnp.where(i < last, 3, 4)
    lo_mask = lax.broadcasted_iota(jnp.int32, (1, LANES), 1) < HEAD_DIM
    n_keys = 3 * BLOCK
    lo_ones = jnp.where(lax.broadcasted_iota(jnp.int32, (n_keys, LANES), 1) < HEAD_DIM, 1.0, 0.0)
    ones_halves = jnp.concatenate([lo_ones, 1.0 - lo_ones], axis=0).astype(BF16)
    lo_f32 = lax.broadcasted_iota(jnp.int32, (A_GROUP * BLOCK, LANES), 1) < HEAD_DIM

    def key_blocks(prev_ref, cur_ref, next_ref, p):
        blocks = [prev_ref[0, p]] + [cur_ref[0, p, BLOCK * s:BLOCK * (s + 1), :] for s in range(n_sub)]
        blocks.append(next_ref[0, p])
        zero = jnp.zeros_like(blocks[0])
        return ([jnp.where(lo_mask, blk, zero) for blk in blocks], [jnp.where(lo_mask, zero, blk) for blk in blocks])

    def window(halves, s):
        return jnp.concatenate(halves[0][s:s + 3] + halves[1][s:s + 3], axis=0)

    k_halves = [key_blocks(kp_ref, kc_ref, kn_ref, p) for p in range(A_KV_HEADS // 2)]
    v_halves = [key_blocks(vp_ref, vc_ref, vn_ref, p) for p in range(A_KV_HEADS // 2)]

    def qk(p, s):
        rows = slice(BLOCK * s, BLOCK * (s + 1))
        lhs3 = jnp.concatenate([q_ref[0, A_GROUP * p + t, rows, :] for t in range(A_GROUP)], axis=0)
        return _nt_dot(lhs3, window(k_halves[p], s))

    def finish(p, s, scores):
        variants = (first_prev_v if s == 0 else 0, 2, last_next_v if s == n_sub - 1 else 3)
        es, sink_terms = [], []
        for hh in range(2):
            g = 2 * p + hh
            sc = [scores[:, n_keys * hh + LANES * j:n_keys * hh + LANES * (j + 1)] + tb_ref[variants[j], g]
                  for j in range(3)]
            sink = sink_ref[g]
            m = jnp.maximum(jnp.max(jnp.maximum(jnp.maximum(sc[0], sc[1]), sc[2]), axis=-1, keepdims=True), sink)
            es += [jnp.exp(sj - m).astype(BF16) for sj in sc]
            sink_terms.append(jnp.exp(sink - m))
        v_aug = jnp.concatenate([window(v_halves[p], s), ones_halves], axis=1)
        o_aug = jnp.dot(jnp.concatenate(es, axis=1), v_aug, preferred_element_type=F32)
        l = o_aug[:, LANES:] + jnp.where(lo_f32, sink_terms[0], sink_terms[1])
        o3 = o_aug[:, :LANES] / l
        for t in range(A_GROUP):
            o_ref[0, A_GROUP * p + t, BLOCK * s:BLOCK * (s + 1), :] = o3[BLOCK * t:BLOCK * (t + 1)].astype(BF16)

    chains = [(p, s) for p in range(A_KV_HEADS // 2) for s in range(n_sub)]
    pending = [qk(*c) for c in chains[:A_LOOKAHEAD]]
    for idx, c in enumerate(chains):
        if idx + A_LOOKAHEAD < len(chains):
            pending.append(qk(*chains[idx + A_LOOKAHEAD]))
        finish(*c, pending.pop(0))
    _memory_heads(qm_ref, mk_ref, mv_ref, o_ref, lo_mask)


def _attn_a(proj, memkv, tb, sink_tab):
    batch, _, seq, _ = proj.shape
    tq = min(A_TQ, seq)
    n_sub = tq // BLOCK
    n_steps = seq // tq
    nblk = seq // BLOCK
    assert nblk >= 2

    def halo(group_block, first):
        idx = (lambda i: jnp.maximum(n_sub * i - 1, 0)) if first else (lambda i: jnp.minimum(n_sub * (i + 1), nblk - 1))
        return pl.BlockSpec((1, 2, BLOCK, LANES), lambda b, i: (b, group_block, idx(i), 0))

    def cur(group_block):
        return pl.BlockSpec((1, 2, tq, LANES), lambda b, i: (b, group_block, i, 0))

    return pl.pallas_call(
        _attn_a_kernel,
        out_shape=jax.ShapeDtypeStruct((batch, CAT_GROUPS, seq, LANES), BF16),
        grid=(batch, n_steps),
        in_specs=[
            pl.BlockSpec((1, TOK_GROUPS, tq, LANES), lambda b, i: (b, 0, i, 0)),
            halo(3, True), cur(3), halo(3, False),
            halo(4, True), cur(4), halo(4, False),
            pl.BlockSpec((1, MEM_GROUPS, tq, LANES), lambda b, i: (b, 5, i, 0)),
            pl.BlockSpec((1, MEM_GROUPS, MEM_LEN, LANES), lambda b, i: (b, 0, 0, 0)),
            pl.BlockSpec((1, MEM_GROUPS, MEM_LEN, LANES), lambda b, i: (b, 1, 0, 0)),
            _const_spec(tb.shape),
            _const_spec(sink_tab.shape),
        ],
        out_specs=pl.BlockSpec((1, CAT_GROUPS, tq, LANES), lambda b, i: (b, 0, i, 0)),
        compiler_params=_params(2),
        name="attn_a",
    )(proj, proj, proj, proj, proj, proj, proj, proj, memkv, memkv, tb, sink_tab)


def _bias_b_kernel(rpb_ref, o_ref):
    variant = pl.program_id(0)
    h = pl.program_id(1)
    n_ri = 2 * NB_ROWS - 1
    n_ci = 2 * NB_COLS - 1
    ri_lo = jnp.where(variant == 0, 0, NB_ROWS // 2 - 1)
    ri_hi = jnp.where(variant == 0, n_ri, NB_ROWS // 2 - 1 + NB_ROWS)
    fill = jnp.where(variant == 0, 0.0, NEG_INF)
    c = lax.broadcasted_iota(jnp.int32, (GRID_W, LANES), 0)
    lane = lax.broadcasted_iota(jnp.int32, (GRID_W, LANES), 1)
    cc = jnp.bitwise_and(lane, GRID_W - 1)
    hi_half = lane >= GRID_W
    cs = jnp.clip(c - NB_COLS // 2, 0, GRID_W - NB_COLS)
    col_valid = (cc >= cs) & (cc < cs + NB_COLS)
    dd = cc - c + NB_COLS - 1

    def body(k, carry):
        ri_l = k - 4
        ri_r = k - 3
        ok_l = (ri_l >= ri_lo) & (ri_l < ri_hi)
        ok_r = (ri_r >= ri_lo) & (ri_r < ri_hi)
        base_l = (h * n_ri + jnp.clip(ri_l, 0, n_ri - 1)) * n_ci
        base_r = (h * n_ri + jnp.clip(ri_r, 0, n_ri - 1)) * n_ci
        acc = jnp.zeros((GRID_W, LANES), F32)
        for d in range(n_ci):
            v_l = jnp.where(ok_l, rpb_ref[base_l + d], fill)
            v_r = jnp.where(ok_r, rpb_ref[base_r + d], fill)
            acc = jnp.where(dd == d, jnp.where(hi_half, v_r, v_l), acc)
        o_ref[0, 0, k] = jnp.where(col_valid, acc, NEG_INF)
        return carry

    lax.fori_loop(0, B_NK, body, 0)


def _bias_b(rpb):
    return pl.pallas_call(
        _bias_b_kernel,
        out_shape=jax.ShapeDtypeStruct((2, B_HEADS, B_NK, GRID_W, LANES), F32),
        grid=(2, B_HEADS),
        in_specs=[pl.BlockSpec(memory_space=pltpu.SMEM)],
        out_specs=pl.BlockSpec((1, 1, B_NK, GRID_W, LANES), lambda v, h: (v, h, 0, 0, 0)),
        compiler_params=_params(2),
        name="bias_b",
    )(rpb.reshape(-1))


def _attn_b_kernel(q_ref, kp_ref, kc_ref, kn_ref, vp_ref, vc_ref, vn_ref, qm_ref, mk_ref, mv_ref, pv_ref, o_ref, *,
                   rows):
    i = pl.program_id(1)
    n_steps = pl.num_programs(1)
    lo_mask = lax.broadcasted_iota(jnp.int32, (1, LANES), 1) < HEAD_DIM
    n_keys = B_KEY_ROWS * GRID_W
    n_cols = B_KEY_ROWS // 2

    def scores_and_values(grp):
        qg = q_ref[0, grp]
        zero = jnp.zeros_like(qg)
        lhs = jnp.concatenate([jnp.where(lo_mask, qg, zero), jnp.where(lo_mask, zero, qg)], axis=0)
        kw = jnp.concatenate([kp_ref[0, grp], kc_ref[0, grp], kn_ref[0, grp]], axis=0)
        vw = jnp.concatenate([vp_ref[0, grp], vc_ref[0, grp], vn_ref[0, grp]], axis=0)
        return _nt_dot(lhs, kw), vw

    def finish(grp, e, vw):
        o, l = _pv_with_rowsum(e, vw)
        o = o / l
        o_ref[0, grp] = jnp.where(lo_mask, o[:B_TOK], o[B_TOK:]).astype(BF16)

    interior = (i >= 1) & (i <= n_steps - 2)

    @pl.when(interior)
    def _():
        for grp in range(TOK_GROUPS):
            s, vw = scores_and_values(grp)
            blocks = []
            for hh in range(2):
                for rl in range(B_ROWS_PER_STEP):
                    t0, t1 = rl // 2, (rl + NB_ROWS - 1) // 2 + 1
                    r0 = hh * B_TOK + rl * GRID_W
                    bias = jnp.concatenate([pv_ref[1, 2 * grp + hh, 2 * t - rl + NB_ROWS - 1] for t in range(t0, t1)],
                                           axis=1)
                    sb = s[r0:r0 + GRID_W, LANES * t0:LANES * t1] + bias
                    e = jnp.exp(sb - jnp.max(sb, axis=-1, keepdims=True)).astype(BF16)
                    parts = [jnp.zeros((GRID_W, LANES * t0), BF16)] if t0 else []
                    parts.append(e)
                    if t1 < n_cols:
                        parts.append(jnp.zeros((GRID_W, LANES * (n_cols - t1)), BF16))
                    blocks.append(jnp.concatenate(parts, axis=1))
            finish(grp, jnp.concatenate(blocks, axis=0), vw)
        _memory_heads(qm_ref, mk_ref, mv_ref, o_ref, lo_mask)

    @pl.when(jnp.logical_not(interior))
    def _():
        ws = B_ROWS_PER_STEP * jnp.clip(i - 1, 0, n_steps - 3)
        key_row = lax.broadcasted_iota(jnp.int32, (1, n_keys), 1) // GRID_W
        table_k0 = []
        row_pen = []
        for rl in range(B_ROWS_PER_STEP):
            r = B_ROWS_PER_STEP * i + rl
            rs = jnp.clip(r - NB_ROWS // 2, 0, rows - NB_ROWS)
            kr0 = rs - ws
            table_k0.append(rs - r - kr0 + NB_ROWS - 1 + 4)
            row_pen.append(jnp.where((key_row >= kr0) & (key_row < kr0 + NB_ROWS), 0.0, NEG_INF).astype(F32))
        for grp in range(TOK_GROUPS):
            s, vw = scores_and_values(grp)
            biases = []
            for hh in range(2):
                for rl in range(B_ROWS_PER_STEP):
                    b = jnp.concatenate([pv_ref[0, 2 * grp + hh, table_k0[rl] + 2 * t] for t in range(n_cols)], axis=1)
                    biases.append(b + row_pen[rl])
            s = s + jnp.concatenate(biases, axis=0)
            m = jnp.max(s, axis=-1, keepdims=True)
            finish(grp, jnp.exp(s - m).astype(BF16), vw)
        _memory_heads(qm_ref, mk_ref, mv_ref, o_ref, lo_mask)


def _attn_b(proj, memkv, pv):
    batch, _, seq, _ = proj.shape
    rows = seq // GRID_W
    n_steps = rows // B_ROWS_PER_STEP
    assert n_steps >= 3 and rows >= NB_ROWS
    kv_blk = (1, TOK_GROUPS, B_TOK, LANES)

    def halo(group_block, pos):
        return pl.BlockSpec(kv_blk, lambda b, i: (b, group_block, jnp.clip(i - 1, 0, n_steps - 3) + pos, 0))

    return pl.pallas_call(
        functools.partial(_attn_b_kernel, rows=rows),
        out_shape=jax.ShapeDtypeStruct((batch, CAT_GROUPS, seq, LANES), BF16),
        grid=(batch, n_steps),
        in_specs=[
            pl.BlockSpec((1, TOK_GROUPS, B_TOK, LANES), lambda b, i: (b, 0, i, 0)),
            halo(1, 0), halo(1, 1), halo(1, 2),
            halo(2, 0), halo(2, 1), halo(2, 2),
            pl.BlockSpec((1, MEM_GROUPS, B_TOK, LANES), lambda b, i: (b, 3 * TOK_GROUPS // MEM_GROUPS, i, 0)),
            pl.BlockSpec((1, MEM_GROUPS, MEM_LEN, LANES), lambda b, i: (b, 0, 0, 0)),
            pl.BlockSpec((1, MEM_GROUPS, MEM_LEN, LANES), lambda b, i: (b, 1, 0, 0)),
            _const_spec(pv.shape),
        ],
        out_specs=pl.BlockSpec((1, CAT_GROUPS, B_TOK, LANES), lambda b, i: (b, 0, i, 0)),
        compiler_params=_params(2),
        name="attn_b",
    )(proj, proj, proj, proj, proj, proj, proj, proj, memkv, memkv, pv)


def _post_kernel(x_ref, cat_ref, wo_ref, wgu_ref, wd_ref, gpost_ref, gpre_ref, gfpost_ref, o_ref):
    chain_rows = x_ref.shape[1] // POST_CHAINS
    for ch in range(POST_CHAINS):
        rows = slice(ch * chain_rows, (ch + 1) * chain_rows)
        cat = jnp.concatenate([cat_ref[0, g, rows, :] for g in range(CAT_GROUPS)], axis=1)
        mixed = jnp.dot(cat, wo_ref[...], preferred_element_type=F32)
        x1 = x_ref[0, rows, :] + _rms(mixed, gpost_ref[...])
        h2 = _rms(x1, gpre_ref[...]).astype(BF16)
        acc = jnp.zeros(x1.shape, F32)
        c0 = 0
        for size in FF_CHUNKS:
            cols = slice(c0, c0 + size)
            gate = jnp.dot(h2, wgu_ref[0, :, cols], preferred_element_type=F32)
            up = jnp.dot(h2, wgu_ref[1, :, cols], preferred_element_type=F32)
            act = (gate * jax.nn.sigmoid(gate) * up).astype(BF16)
            acc = acc + jnp.dot(act, wd_ref[cols, :], preferred_element_type=F32)
            c0 += size
        o_ref[0, rows, :] = x1 + _rms(acc, gfpost_ref[...])


def _post(x, cat, w_out, w_gu, w_down, g_post, g_pre, g_fpost):
    batch, seq, _ = x.shape
    tm = min(TM_DENSE, seq)
    gain = lambda g: g.reshape(1, D_MODEL)
    x_spec = pl.BlockSpec((1, tm, D_MODEL), lambda b, i: (b, i, 0))
    return pl.pallas_call(
        _post_kernel,
        out_shape=jax.ShapeDtypeStruct(x.shape, F32),
        grid=(batch, seq // tm),
        in_specs=[
            x_spec,
            pl.BlockSpec((1, CAT_GROUPS, tm, LANES), lambda b, i: (b, 0, i, 0)),
            _const_spec(w_out.shape),
            _const_spec(w_gu.shape),
            _const_spec(w_down.shape),
            _const_spec((1, D_MODEL)),
            _const_spec((1, D_MODEL)),
            _const_spec((1, D_MODEL)),
        ],
        out_specs=x_spec,
        compiler_params=_params(2),
        name="post_ffn",
    )(x, cat, w_out, w_gu, w_down, gain(g_post), gain(g_pre), gain(g_fpost))


def _head_cols(order):
    return np.concatenate([np.arange(HEAD_DIM) + HEAD_DIM * h for h in order])


def _prep_w_in_a(w):
    q_cols = _head_cols(A_HEAD_ORDER)
    cols = np.concatenate([q_cols, np.arange(A_Q_HEADS * HEAD_DIM, w.shape[1])])
    return w[:, cols].astype(BF16)


def _prep_w_out_a(w):
    rows = np.concatenate([_head_cols(A_HEAD_ORDER), np.arange(A_Q_HEADS * HEAD_DIM, w.shape[0])])
    return w[rows, :].astype(BF16)


def _prep_w_gu(w):
    return jnp.stack([w[:, :D_FF], w[:, D_FF:]]).astype(BF16)


def kernel(x, mem, w_in_a, sink_a, w_in_b, rpb_b, t5_table, w_mem_kv, w_out, w_gu, w_down, norm_mix_pre,
           norm_mix_post, norm_mem, norm_ffn_pre, norm_ffn_post):
    depth = w_out.shape[0]
    memkv = _memkv(mem, norm_mem, w_mem_kv.astype(BF16))
    tb = _bias_a(t5_table)
    a_scaled = frozenset(range(TOK_GROUPS)) | frozenset(range(A_GROUPS - MEM_GROUPS, A_GROUPS))
    b_scaled = frozenset(range(TOK_GROUPS)) | frozenset(range(B_GROUPS - MEM_GROUPS, B_GROUPS))
    for i in range(depth):
        j = i // 2
        if i % 2 == 0:
            proj = _inproj(x, norm_mix_pre[i], _prep_w_in_a(w_in_a[j]), a_scaled)
            sink_tab = jnp.broadcast_to(sink_a[j].astype(F32).reshape(A_KV_HEADS, A_GROUP, 1, 1),
                                        (A_KV_HEADS, A_GROUP, BLOCK, LANES)).reshape(A_KV_HEADS, A_GROUP * BLOCK, LANES)
            cat = _attn_a(proj, memkv[i], tb, sink_tab)
            wo = _prep_w_out_a(w_out[i])
        else:
            proj = _inproj(x, norm_mix_pre[i], w_in_b[j].astype(BF16), b_scaled)
            cat = _attn_b(proj, memkv[i], _bias_b(rpb_b[j]))
            wo = w_out[i].astype(BF16)
        x = _post(x, cat, wo, _prep_w_gu(w_gu[i]), w_down[i].astype(BF16),
                  norm_mix_post[i], norm_ffn_pre[i], norm_ffn_post[i])
    return x
```

```python
import functools
import math

import jax
import jax.numpy as jnp
import numpy as np
from jax import lax
from jax.experimental import pallas as pl
from jax.experimental.pallas import tpu as pltpu

F32 = jnp.float32
BF16 = jnp.bfloat16

D_MODEL = 1024
HEAD_DIM = 64
LANES = 128
A_Q_HEADS = 12
A_KV_HEADS = 4
A_GROUP = A_Q_HEADS // A_KV_HEADS
WINDOW = 128
BLOCK = 128
B_HEADS = 12
GRID_W = 64
NB_ROWS = 8
NB_COLS = 16
MEM_HEADS = 4
MEM_LEN = 256
N_BUCKETS = 32
MAX_DISTANCE = 128
D_FF = 2816
RMS_EPS = 1e-6
LOG2E = math.log2(math.e)
Q_SCALE = HEAD_DIM ** -0.5 * LOG2E

A_GROUPS = (A_Q_HEADS + 2 * A_KV_HEADS + MEM_HEADS) * HEAD_DIM // LANES
B_GROUPS = (3 * B_HEADS + MEM_HEADS) * HEAD_DIM // LANES
CAT_GROUPS = (A_Q_HEADS + MEM_HEADS) * HEAD_DIM // LANES
TOK_GROUPS = A_Q_HEADS * HEAD_DIM // LANES
MEM_GROUPS = MEM_HEADS * HEAD_DIM // LANES

A_HEAD_ORDER = tuple(3 * (2 * p + hh) + t for p in range(2) for t in range(3) for hh in range(2))

TM_DENSE = 1024
POST_CHAINS = 2
FF_CHUNKS = (256,) * 11
assert sum(FF_CHUNKS) == D_FF
A_TQ = 512
A_LOOKAHEAD = 2
B_ROWS_PER_STEP = 4
B_TOK = B_ROWS_PER_STEP * GRID_W
B_KEY_ROWS = 3 * B_ROWS_PER_STEP
B_NK = 22
NEG_INF = float("-inf")
VMEM_LIMIT = 56 * 1024 * 1024


def _rms(x, g):
    y = x * lax.rsqrt(jnp.mean(x * x, axis=-1, keepdims=True) + RMS_EPS)
    return y * g


def _nt_dot(a, b):
    return lax.dot_general(a, b, (((1,), (1,)), ((), ())), preferred_element_type=F32)


def _const_spec(shape):
    n = len(shape)
    return pl.BlockSpec(shape, lambda *_: (0,) * n, pipeline_mode=pl.Buffered(1))


def _params(n_axes, flags=None):
    return pltpu.CompilerParams(dimension_semantics=("arbitrary",) * n_axes, vmem_limit_bytes=VMEM_LIMIT,
                                flags=flags)


def _memkv_kernel(mem_ref, g_ref, w_ref, o_ref):
    m = _rms(mem_ref[0], g_ref[0]).astype(BF16)
    res = jnp.dot(m, w_ref[0], preferred_element_type=F32)
    for g in range(2 * MEM_GROUPS):
        o_ref[0, 0, g] = res[:, g * LANES:(g + 1) * LANES].astype(BF16)


def _memkv(mem, norm_mem, w_mem_kv):
    depth, batch = w_mem_kv.shape[0], mem.shape[0]
    return pl.pallas_call(
        _memkv_kernel,
        out_shape=jax.ShapeDtypeStruct((depth, batch, 2 * MEM_GROUPS, MEM_LEN, LANES), BF16),
        grid=(depth, batch),
        in_specs=[
            pl.BlockSpec((1, MEM_LEN, D_MODEL), lambda l, b: (b, 0, 0)),
            pl.BlockSpec((1, 1, D_MODEL), lambda l, b: (l, 0, 0)),
            pl.BlockSpec((1, D_MODEL, 2 * MEM_GROUPS * LANES), lambda l, b: (l, 0, 0)),
        ],
        out_specs=pl.BlockSpec((1, 1, 2 * MEM_GROUPS, MEM_LEN, LANES), lambda l, b: (l, b, 0, 0, 0)),
        compiler_params=_params(2),
        name="mem_kv",
    )(mem, norm_mem.reshape(depth, 1, D_MODEL), w_mem_kv)


def _store_groups(o_ref, rows, res, scaled_groups):
    for g in range(o_ref.shape[1]):
        blk = res[:, g * LANES:(g + 1) * LANES]
        if g in scaled_groups:
            blk = blk * Q_SCALE
        o_ref[0, g, rows, :] = blk.astype(BF16)


def _inproj_kernel(x_ref, g_ref, w_ref, o_ref, *, scaled_groups):
    h = _rms(x_ref[0], g_ref[...]).astype(BF16)
    res = jnp.dot(h, w_ref[...], preferred_element_type=F32)
    _store_groups(o_ref, slice(None), res, scaled_groups)


def _inproj(x, gain, w, scaled_groups):
    batch, seq, _ = x.shape
    n_groups = w.shape[1] // LANES
    tm = min(TM_DENSE, seq)
    return pl.pallas_call(
        functools.partial(_inproj_kernel, scaled_groups=scaled_groups),
        out_shape=jax.ShapeDtypeStruct((batch, n_groups, seq, LANES), BF16),
        grid=(batch, seq // tm),
        in_specs=[
            pl.BlockSpec((1, tm, D_MODEL), lambda b, i: (b, i, 0)),
            _const_spec((1, D_MODEL)),
            _const_spec(w.shape),
        ],
        out_specs=pl.BlockSpec((1, n_groups, tm, LANES), lambda b, i: (b, 0, i, 0)),
        compiler_params=_params(2),
        name="in_proj",
    )(x, gain.reshape(1, D_MODEL), w)


def _split_halves(x, lo_mask):
    zero = jnp.zeros_like(x)
    return jnp.concatenate([jnp.where(lo_mask, x, zero), jnp.where(lo_mask, zero, x)], axis=0)


def _pv_with_rowsum(e_bf16, v):
    v_aug = jnp.concatenate([v, jnp.ones_like(v)], axis=1)
    o_aug = jnp.dot(e_bf16, v_aug, preferred_element_type=F32)
    return o_aug[:, :LANES], o_aug[:, LANES:]


def _memory_heads(qm_ref, mk_ref, mv_ref, o_ref, lo_mask):
    tq = qm_ref.shape[2]
    for grp in range(MEM_GROUPS):
        qg = qm_ref[0, grp]
        zero = jnp.zeros_like(qg)
        lhs = jnp.concatenate([jnp.where(lo_mask, qg, zero), jnp.where(lo_mask, zero, qg)], axis=0)
        s = _nt_dot(lhs, mk_ref[0, grp])
        m = jnp.max(s, axis=-1, keepdims=True)
        e = jnp.exp2(s - m).astype(BF16)
        o, l = _pv_with_rowsum(e, mv_ref[0, grp])
        o = o / l
        o_ref[0, TOK_GROUPS + grp] = jnp.where(lo_mask, o[:tq], o[tq:]).astype(BF16)


def _t5_bucket(rel):
    half = N_BUCKETS // 2
    max_exact = half // 2
    n = -rel
    ret = jnp.where(n < 0, half, 0)
    n = jnp.abs(n)
    nf = jnp.maximum(n, 1).astype(jnp.float32)
    large = max_exact + (jnp.log(nf / max_exact) / math.log(MAX_DISTANCE / max_exact)
                         * (half - max_exact)).astype(jnp.int32)
    large = jnp.minimum(large, half - 1)
    return ret + jnp.where(n < max_exact, n, large)


def _bias_a_kernel(t5_ref, bucket_ref, o_ref):
    q = lax.broadcasted_iota(jnp.int32, (BLOCK, BLOCK), 0)
    j = lax.broadcasted_iota(jnp.int32, (BLOCK, BLOCK), 1)
    masked = jnp.full((BLOCK, BLOCK), NEG_INF, F32)
    valid = (j >= q, None, j <= q)
    for g in range(A_KV_HEADS):
        for t in range(A_GROUP):
            head = A_GROUP * g + t
            rows = slice(BLOCK * t, BLOCK * (t + 1))
            for kb, variant in ((0, 0), (1, 2), (2, 3)):
                bucket = bucket_ref[kb]
                acc = jnp.zeros((BLOCK, BLOCK), F32)
                for b in range(N_BUCKETS):
                    acc = jnp.where(bucket == b, t5_ref[b * A_Q_HEADS + head], acc)
                acc = acc * LOG2E
                if valid[kb] is not None:
                    acc = jnp.where(valid[kb], acc, masked)
                o_ref[variant, g, rows, :] = acc
            o_ref[1, g, rows, :] = masked
            o_ref[4, g, rows, :] = masked


def _bias_a(t5_table):
    q = jnp.arange(BLOCK)[:, None]
    j = jnp.arange(BLOCK)[None, :]
    rel = jnp.stack([j - WINDOW - q, j - q, j + WINDOW - q])
    bucket = _t5_bucket(rel).astype(jnp.int32)
    return pl.pallas_call(
        _bias_a_kernel,
        out_shape=jax.ShapeDtypeStruct((5, A_KV_HEADS, A_GROUP * BLOCK, BLOCK), F32),
        in_specs=[pl.BlockSpec(memory_space=pltpu.SMEM), pl.BlockSpec(memory_space=pltpu.VMEM)],
        out_specs=pl.BlockSpec(memory_space=pltpu.VMEM),
        name="bias_a",
    )(t5_table.reshape(-1), bucket)


def _attn_a_kernel(q_ref, kp_ref, kc_ref, kn_ref, vp_ref, vc_ref, vn_ref, qm_ref, mk_ref, mv_ref, tb_ref,
                   sink_ref, o_ref):
    n_sub = q_ref.shape[2] // BLOCK
    i = pl.program_id(1)
    last = pl.num_programs(1) - 1
    first_prev_v = jnp.where(i > 0, 0, 1)
    last_next_v = jnp.where(i < last, 3, 4)
    lo_mask = lax.broadcasted_iota(jnp.int32, (1, LANES), 1) < HEAD_DIM
    n_keys = 3 * BLOCK
    lo_ones = jnp.where(lax.broadcasted_iota(jnp.int32, (n_keys, LANES), 1) < HEAD_DIM, 1.0, 0.0)
    ones_halves = jnp.concatenate([lo_ones, 1.0 - lo_ones], axis=0).astype(BF16)
    lo_f32 = lax.broadcasted_iota(jnp.int32, (A_GROUP * BLOCK, LANES), 1) < HEAD_DIM

    def key_blocks(prev_ref, cur_ref, next_ref, p):
        blocks = [prev_ref[0, p]] + [cur_ref[0, p, BLOCK * s:BLOCK * (s + 1), :] for s in range(n_sub)]
        blocks.append(next_ref[0, p])
        zero = jnp.zeros_like(blocks[0])
        return ([jnp.where(lo_mask, blk, zero) for blk in blocks], [jnp.where(lo_mask, zero, blk) for blk in blocks])

    def window(halves, s):
        return jnp.concatenate(halves[0][s:s + 3] + halves[1][s:s + 3], axis=0)

    k_halves = [key_blocks(kp_ref, kc_ref, kn_ref, p) for p in range(A_KV_HEADS // 2)]
    v_halves = [key_blocks(vp_ref, vc_ref, vn_ref, p) for p in range(A_KV_HEADS // 2)]

    def qk(p, s):
        rows = slice(BLOCK * s, BLOCK * (s + 1))
        lhs3 = jnp.concatenate([q_ref[0, A_GROUP * p + t, rows, :] for t in range(A_GROUP)], axis=0)
        return _nt_dot(lhs3, window(k_halves[p], s))

    def finish(p, s, scores):
        variants = (first_prev_v if s == 0 else 0, 2, last_next_v if s == n_sub - 1 else 3)
        es, sink_terms = [], []
        for hh in range(2):
            g = 2 * p + hh
            sc = [scores[:, n_keys * hh + LANES * j:n_keys * hh + LANES * (j + 1)] + tb_ref[variants[j], g]
                  for j in range(3)]
            sink = sink_ref[g]
            m = jnp.maximum(jnp.max(jnp.maximum(jnp.maximum(sc[0], sc[1]), sc[2]), axis=-1, keepdims=True), sink)
            es += [jnp.exp2(sj - m).astype(BF16) for sj in sc]
            sink_terms.append(jnp.exp2(sink - m))
        v_aug = jnp.concatenate([window(v_halves[p], s), ones_halves], axis=1)
        o_aug = jnp.dot(jnp.concatenate(es, axis=1), v_aug, preferred_element_type=F32)
        l = o_aug[:, LANES:] + jnp.where(lo_f32, sink_terms[0], sink_terms[1])
        o3 = o_aug[:, :LANES] / l
        for t in range(A_GROUP):
            o_ref[0, A_GROUP * p + t, BLOCK * s:BLOCK * (s + 1), :] = o3[BLOCK * t:BLOCK * (t + 1)].astype(BF16)

    chains = [(p, s) for p in range(A_KV_HEADS // 2) for s in range(n_sub)]
    pending = [qk(*c) for c in chains[:A_LOOKAHEAD]]
    for idx, c in enumerate(chains):
        if idx + A_LOOKAHEAD < len(chains):
            pending.append(qk(*chains[idx + A_LOOKAHEAD]))
        finish(*c, pending.pop(0))
    _memory_heads(qm_ref, mk_ref, mv_ref, o_ref, lo_mask)


def _attn_a(proj, memkv, tb, sink_tab):
    batch, _, seq, _ = proj.shape
    tq = min(A_TQ, seq)
    n_sub = tq // BLOCK
    n_steps = seq // tq
    nblk = seq // BLOCK
    assert nblk >= 2

    def halo(group_block, first):
        idx = (lambda i: jnp.maximum(n_sub * i - 1, 0)) if first else (lambda i: jnp.minimum(n_sub * (i + 1), nblk - 1))
        return pl.BlockSpec((1, 2, BLOCK, LANES), lambda b, i: (b, group_block, idx(i), 0))

    def cur(group_block):
        return pl.BlockSpec((1, 2, tq, LANES), lambda b, i: (b, group_block, i, 0))

    return pl.pallas_call(
        _attn_a_kernel,
        out_shape=jax.ShapeDtypeStruct((batch, CAT_GROUPS, seq, LANES), BF16),
        grid=(batch, n_steps),
        in_specs=[
            pl.BlockSpec((1, TOK_GROUPS, tq, LANES), lambda b, i: (b, 0, i, 0)),
            halo(3, True), cur(3), halo(3, False),
            halo(4, True), cur(4), halo(4, False),
            pl.BlockSpec((1, MEM_GROUPS, tq, LANES), lambda b, i: (b, 5, i, 0)),
            pl.BlockSpec((1, MEM_GROUPS, MEM_LEN, LANES), lambda b, i: (b, 0, 0, 0)),
            pl.BlockSpec((1, MEM_GROUPS, MEM_LEN, LANES), lambda b, i: (b, 1, 0, 0)),
            _const_spec(tb.shape),
            _const_spec(sink_tab.shape),
        ],
        out_specs=pl.BlockSpec((1, CAT_GROUPS, tq, LANES), lambda b, i: (b, 0, i, 0)),
        compiler_params=_params(2),
        name="attn_a",
    )(proj, proj, proj, proj, proj, proj, proj, proj, memkv, memkv, tb, sink_tab)


def _bias_b_kernel(rpb_ref, o_ref):
    h = pl.program_id(0)
    n_ri = 2 * NB_ROWS - 1
    n_ci = 2 * NB_COLS - 1
    int_lo = NB_ROWS // 2 - 1
    int_hi = int_lo + NB_ROWS
    c = lax.broadcasted_iota(jnp.int32, (GRID_W, LANES), 0)
    lane = lax.broadcasted_iota(jnp.int32, (GRID_W, LANES), 1)
    cc = jnp.bitwise_and(lane, GRID_W - 1)
    hi_half = lane >= GRID_W
    cs = jnp.clip(c - NB_COLS // 2, 0, GRID_W - NB_COLS)
    col_valid = (cc >= cs) & (cc < cs + NB_COLS)
    dd = cc - c + NB_COLS - 1

    def body(k, carry):
        ri_l = k - 4
        ri_r = k - 3
        ok_l = (ri_l >= 0) & (ri_l < n_ri)
        ok_r = (ri_r >= 0) & (ri_r < n_ri)
        base_l = (h * n_ri + jnp.clip(ri_l, 0, n_ri - 1)) * n_ci
        base_r = (h * n_ri + jnp.clip(ri_r, 0, n_ri - 1)) * n_ci
        acc = jnp.zeros((GRID_W, LANES), F32)
        for d in range(n_ci):
            v_l = jnp.where(ok_l, rpb_ref[base_l + d], 0.0)
            v_r = jnp.where(ok_r, rpb_ref[base_r + d], 0.0)
            acc = jnp.where(dd == d, jnp.where(hi_half, v_r, v_l), acc)
        tile = jnp.where(col_valid, acc * LOG2E, NEG_INF)
        o_ref[0, 0, k] = tile
        interior_l = (ri_l >= int_lo) & (ri_l < int_hi)
        interior_r = (ri_r >= int_lo) & (ri_r < int_hi)
        o_ref[1, 0, k] = jnp.where(hi_half, jnp.where(interior_r, tile, NEG_INF), jnp.where(interior_l, tile, NEG_INF))
        return carry

    lax.fori_loop(0, B_NK, body, 0)


def _bias_b(rpb):
    return pl.pallas_call(
        _bias_b_kernel,
        out_shape=jax.ShapeDtypeStruct((2, B_HEADS, B_NK, GRID_W, LANES), F32),
        grid=(B_HEADS,),
        in_specs=[pl.BlockSpec(memory_space=pltpu.SMEM)],
        out_specs=pl.BlockSpec((2, 1, B_NK, GRID_W, LANES), lambda h: (0, h, 0, 0, 0)),
        compiler_params=_params(1),
        name="bias_b",
    )(rpb.reshape(-1))


def _attn_b_kernel(q_ref, kp_ref, kc_ref, kn_ref, vp_ref, vc_ref, vn_ref, qm_ref, mk_ref, mv_ref, pv_ref, o_ref, *,
                   rows):
    i = pl.program_id(1)
    n_steps = pl.num_programs(1)
    lo_mask = lax.broadcasted_iota(jnp.int32, (1, LANES), 1) < HEAD_DIM
    n_keys = B_KEY_ROWS * GRID_W
    n_cols = B_KEY_ROWS // 2

    def scores_and_values(grp):
        qg = q_ref[0, grp]
        zero = jnp.zeros_like(qg)
        lhs = jnp.concatenate([jnp.where(lo_mask, qg, zero), jnp.where(lo_mask, zero, qg)], axis=0)
        kw = jnp.concatenate([kp_ref[0, grp], kc_ref[0, grp], kn_ref[0, grp]], axis=0)
        vw = jnp.concatenate([vp_ref[0, grp], vc_ref[0, grp], vn_ref[0, grp]], axis=0)
        return _nt_dot(lhs, kw), vw

    def finish(grp, e, vw):
        o, l = _pv_with_rowsum(e, vw)
        o = o / l
        o_ref[0, grp] = jnp.where(lo_mask, o[:B_TOK], o[B_TOK:]).astype(BF16)

    interior = (i >= 1) & (i <= n_steps - 2)

    @pl.when(interior)
    def _():
        for grp in range(TOK_GROUPS):
            s, vw = scores_and_values(grp)
            blocks = []
            for hh in range(2):
                for rl in range(B_ROWS_PER_STEP):
                    t0, t1 = rl // 2, (rl + NB_ROWS - 1) // 2 + 1
                    r0 = hh * B_TOK + rl * GRID_W
                    bias = jnp.concatenate([pv_ref[1, 2 * grp + hh, 2 * t - rl + NB_ROWS - 1] for t in range(t0, t1)],
                                           axis=1)
                    sb = s[r0:r0 + GRID_W, LANES * t0:LANES * t1] + bias
                    e = jnp.exp2(sb - jnp.max(sb, axis=-1, keepdims=True)).astype(BF16)
                    parts = [jnp.zeros((GRID_W, LANES * t0), BF16)] if t0 else []
                    parts.append(e)
                    if t1 < n_cols:
                        parts.append(jnp.zeros((GRID_W, LANES * (n_cols - t1)), BF16))
                    blocks.append(jnp.concatenate(parts, axis=1))
            finish(grp, jnp.concatenate(blocks, axis=0), vw)
        _memory_heads(qm_ref, mk_ref, mv_ref, o_ref, lo_mask)

    @pl.when(jnp.logical_not(interior))
    def _():
        ws = B_ROWS_PER_STEP * jnp.clip(i - 1, 0, n_steps - 3)
        key_row = lax.broadcasted_iota(jnp.int32, (1, n_keys), 1) // GRID_W
        table_k0 = []
        row_pen = []
        for rl in range(B_ROWS_PER_STEP):
            r = B_ROWS_PER_STEP * i + rl
            rs = jnp.clip(r - NB_ROWS // 2, 0, rows - NB_ROWS)
            kr0 = rs - ws
            table_k0.append(rs - r - kr0 + NB_ROWS - 1 + 4)
            row_pen.append(jnp.where((key_row >= kr0) & (key_row < kr0 + NB_ROWS), 0.0, NEG_INF).astype(F32))
        for grp in range(TOK_GROUPS):
            s, vw = scores_and_values(grp)
            biases = []
            for hh in range(2):
                for rl in range(B_ROWS_PER_STEP):
                    b = jnp.concatenate([pv_ref[0, 2 * grp + hh, table_k0[rl] + 2 * t] for t in range(n_cols)], axis=1)
                    biases.append(b + row_pen[rl])
            s = s + jnp.concatenate(biases, axis=0)
            m = jnp.max(s, axis=-1, keepdims=True)
            finish(grp, jnp.exp2(s - m).astype(BF16), vw)
        _memory_heads(qm_ref, mk_ref, mv_ref, o_ref, lo_mask)


def _attn_b(proj, memkv, pv):
    batch, _, seq, _ = proj.shape
    rows = seq // GRID_W
    n_steps = rows // B_ROWS_PER_STEP
    assert n_steps >= 3 and rows >= NB_ROWS
    kv_blk = (1, TOK_GROUPS, B_TOK, LANES)

    def halo(group_block, pos):
        return pl.BlockSpec(kv_blk, lambda b, i: (b, group_block, jnp.clip(i - 1, 0, n_steps - 3) + pos, 0))

    return pl.pallas_call(
        functools.partial(_attn_b_kernel, rows=rows),
        out_shape=jax.ShapeDtypeStruct((batch, CAT_GROUPS, seq, LANES), BF16),
        grid=(batch, n_steps),
        in_specs=[
            pl.BlockSpec((1, TOK_GROUPS, B_TOK, LANES), lambda b, i: (b, 0, i, 0)),
            halo(1, 0), halo(1, 1), halo(1, 2),
            halo(2, 0), halo(2, 1), halo(2, 2),
            pl.BlockSpec((1, MEM_GROUPS, B_TOK, LANES), lambda b, i: (b, 3 * TOK_GROUPS // MEM_GROUPS, i, 0)),
            pl.BlockSpec((1, MEM_GROUPS, MEM_LEN, LANES), lambda b, i: (b, 0, 0, 0)),
            pl.BlockSpec((1, MEM_GROUPS, MEM_LEN, LANES), lambda b, i: (b, 1, 0, 0)),
            _const_spec(pv.shape),
        ],
        out_specs=pl.BlockSpec((1, CAT_GROUPS, B_TOK, LANES), lambda b, i: (b, 0, i, 0)),
        compiler_params=_params(2),
        name="attn_b",
    )(proj, proj, proj, proj, proj, proj, proj, proj, memkv, memkv, pv)


def _post_kernel(x_ref, cat_ref, wo_ref, wgu_ref, wd_ref, gpost_ref, gpre_ref, gfpost_ref, o_ref):
    chain_rows = x_ref.shape[1] // POST_CHAINS
    for ch in range(POST_CHAINS):
        rows = slice(ch * chain_rows, (ch + 1) * chain_rows)
        cat = jnp.concatenate([cat_ref[0, g, rows, :] for g in range(CAT_GROUPS)], axis=1)
        mixed = jnp.dot(cat, wo_ref[...], preferred_element_type=F32)
        x1 = x_ref[0, rows, :] + _rms(mixed, gpost_ref[...])
        h2 = _rms(x1, gpre_ref[...]).astype(BF16)
        acc = jnp.zeros(x1.shape, F32)
        c0 = 0
        for size in FF_CHUNKS:
            cols = slice(c0, c0 + size)
            gate = jnp.dot(h2, wgu_ref[0, :, cols], preferred_element_type=F32)
            up = jnp.dot(h2, wgu_ref[1, :, cols], preferred_element_type=F32)
            act = (gate * jax.nn.sigmoid(gate) * up).astype(BF16)
            acc = acc + jnp.dot(act, wd_ref[cols, :], preferred_element_type=F32)
            c0 += size
        o_ref[0, rows, :] = x1 + _rms(acc, gfpost_ref[...])


def _post(x, cat, w_out, w_gu, w_down, g_post, g_pre, g_fpost):
    batch, seq, _ = x.shape
    tm = min(TM_DENSE, seq)
    gain = lambda g: g.reshape(1, D_MODEL)
    x_spec = pl.BlockSpec((1, tm, D_MODEL), lambda b, i: (b, i, 0))
    return pl.pallas_call(
        _post_kernel,
        out_shape=jax.ShapeDtypeStruct(x.shape, F32),
        grid=(batch, seq // tm),
        in_specs=[
            x_spec,
            pl.BlockSpec((1, CAT_GROUPS, tm, LANES), lambda b, i: (b, 0, i, 0)),
            _const_spec(w_out.shape),
            _const_spec(w_gu.shape),
            _const_spec(w_down.shape),
            _const_spec((1, D_MODEL)),
            _const_spec((1, D_MODEL)),
            _const_spec((1, D_MODEL)),
        ],
        out_specs=x_spec,
        compiler_params=_params(2),
        name="post_ffn",
    )(x, cat, w_out, w_gu, w_down, gain(g_post), gain(g_pre), gain(g_fpost))


def _head_cols(order):
    return np.concatenate([np.arange(HEAD_DIM) + HEAD_DIM * h for h in order])


def _prep_w_in_a(w):
    q_cols = _head_cols(A_HEAD_ORDER)
    cols = np.concatenate([q_cols, np.arange(A_Q_HEADS * HEAD_DIM, w.shape[1])])
    return w[:, cols].astype(BF16)


def _prep_w_out_a(w):
    rows = np.concatenate([_head_cols(A_HEAD_ORDER), np.arange(A_Q_HEADS * HEAD_DIM, w.shape[0])])
    return w[rows, :].astype(BF16)


def _prep_w_gu(w):
    return jnp.stack([w[:, :D_FF], w[:, D_FF:]]).astype(BF16)


def kernel(x, mem, w_in_a, sink_a, w_in_b, rpb_b, t5_table, w_mem_kv, w_out, w_gu, w_down, norm_mix_pre,
           norm_mix_post, norm_mem, norm_ffn_pre, norm_ffn_post):
    depth = w_out.shape[0]
    memkv = _memkv(mem, norm_mem, w_mem_kv.astype(BF16))
    tb = _bias_a(t5_table)
    a_scaled = frozenset(range(TOK_GROUPS)) | frozenset(range(A_GROUPS - MEM_GROUPS, A_GROUPS))
    b_scaled = frozenset(range(TOK_GROUPS)) | frozenset(range(B_GROUPS - MEM_GROUPS, B_GROUPS))
    for i in range(depth):
        j = i // 2
        if i % 2 == 0:
            proj = _inproj(x, norm_mix_pre[i], _prep_w_in_a(w_in_a[j]), a_scaled)
            sink_tab = jnp.broadcast_to((sink_a[j].astype(F32) * LOG2E).reshape(A_KV_HEADS, A_GROUP, 1, 1),
                                        (A_KV_HEADS, A_GROUP, BLOCK, LANES)).reshape(A_KV_HEADS, A_GROUP * BLOCK, LANES)
            cat = _attn_a(proj, memkv[i], tb, sink_tab)
            wo = _prep_w_out_a(w_out[i])
        else:
            proj = _inproj(x, norm_mix_pre[i], w_in_b[j].astype(BF16), b_scaled)
            cat = _attn_b(proj, memkv[i], _bias_b(rpb_b[j]))
            wo = w_out[i].astype(BF16)
        x = _post(x, cat, wo, _prep_w_gu(w_gu[i]), w_down[i].astype(BF16),
                  norm_mix_post[i], norm_ffn_pre[i], norm_ffn_post[i])
    return x
```

```python
import functools
import math

import jax
import jax.numpy as jnp
import numpy as np
from jax import lax
from jax.experimental import pallas as pl
from jax.experimental.pallas import tpu as pltpu

F32 = jnp.float32
BF16 = jnp.bfloat16

D_MODEL = 1024
HEAD_DIM = 64
LANES = 128
A_Q_HEADS = 12
A_KV_HEADS = 4
A_GROUP = A_Q_HEADS // A_KV_HEADS
WINDOW = 128
BLOCK = 128
B_HEADS = 12
GRID_W = 64
NB_ROWS = 8
NB_COLS = 16
MEM_HEADS = 4
MEM_LEN = 256
N_BUCKETS = 32
MAX_DISTANCE = 128
D_FF = 2816
RMS_EPS = 1e-6
LOG2E = math.log2(math.e)
Q_SCALE = HEAD_DIM ** -0.5 * LOG2E

A_GROUPS = (A_Q_HEADS + 2 * A_KV_HEADS + MEM_HEADS) * HEAD_DIM // LANES
B_GROUPS = (3 * B_HEADS + MEM_HEADS) * HEAD_DIM // LANES
CAT_GROUPS = (A_Q_HEADS + MEM_HEADS) * HEAD_DIM // LANES
TOK_GROUPS = A_Q_HEADS * HEAD_DIM // LANES
MEM_GROUPS = MEM_HEADS * HEAD_DIM // LANES

A_HEAD_ORDER = tuple(3 * (2 * p + hh) + t for p in range(2) for t in range(3) for hh in range(2))

TM_DENSE = 1024
POST_CHAINS = 2
FF_CHUNKS = (256,) * 11
assert sum(FF_CHUNKS) == D_FF
A_TQ = 512
B_ROWS_PER_STEP = 4
B_TOK = B_ROWS_PER_STEP * GRID_W
B_KEY_ROWS = 3 * B_ROWS_PER_STEP
B_NK = 22
NEG_INF = float("-inf")
VMEM_LIMIT = 56 * 1024 * 1024


def _rms(x, g):
    y = x * lax.rsqrt(jnp.mean(x * x, axis=-1, keepdims=True) + RMS_EPS)
    return y * g


def _nt_dot(a, b):
    return lax.dot_general(a, b, (((1,), (1,)), ((), ())), preferred_element_type=F32)


def _const_spec(shape):
    n = len(shape)
    return pl.BlockSpec(shape, lambda *_: (0,) * n, pipeline_mode=pl.Buffered(1))


def _params(n_axes, flags=None):
    return pltpu.CompilerParams(dimension_semantics=("arbitrary",) * n_axes, vmem_limit_bytes=VMEM_LIMIT,
                                flags=flags)


def _memkv_kernel(mem_ref, g_ref, w_ref, o_ref):
    m = _rms(mem_ref[0], g_ref[0]).astype(BF16)
    res = jnp.dot(m, w_ref[0], preferred_element_type=F32)
    for g in range(2 * MEM_GROUPS):
        o_ref[0, 0, g] = res[:, g * LANES:(g + 1) * LANES].astype(BF16)


def _memkv(mem, norm_mem, w_mem_kv):
    depth, batch = w_mem_kv.shape[0], mem.shape[0]
    return pl.pallas_call(
        _memkv_kernel,
        out_shape=jax.ShapeDtypeStruct((depth, batch, 2 * MEM_GROUPS, MEM_LEN, LANES), BF16),
        grid=(depth, batch),
        in_specs=[
            pl.BlockSpec((1, MEM_LEN, D_MODEL), lambda l, b: (b, 0, 0)),
            pl.BlockSpec((1, 1, D_MODEL), lambda l, b: (l, 0, 0)),
            pl.BlockSpec((1, D_MODEL, 2 * MEM_GROUPS * LANES), lambda l, b: (l, 0, 0)),
        ],
        out_specs=pl.BlockSpec((1, 1, 2 * MEM_GROUPS, MEM_LEN, LANES), lambda l, b: (l, b, 0, 0, 0)),
        compiler_params=_params(2),
        name="mem_kv",
    )(mem, norm_mem.reshape(depth, 1, D_MODEL), w_mem_kv)


def _store_groups(o_ref, rows, res, scaled_groups):
    for g in range(o_ref.shape[1]):
        blk = res[:, g * LANES:(g + 1) * LANES]
        if g in scaled_groups:
            blk = blk * Q_SCALE
        o_ref[0, g, rows, :] = blk.astype(BF16)


def _inproj_kernel(x_ref, g_ref, w_ref, o_ref, *, scaled_groups):
    chain_rows = x_ref.shape[1] // POST_CHAINS
    for ch in range(POST_CHAINS):
        rows = slice(ch * chain_rows, (ch + 1) * chain_rows)
        h = _rms(x_ref[0, rows, :], g_ref[...]).astype(BF16)
        _store_groups(o_ref, rows, jnp.dot(h, w_ref[...], preferred_element_type=F32), scaled_groups)


def _inproj(x, gain, w, scaled_groups):
    batch, seq, _ = x.shape
    n_groups = w.shape[1] // LANES
    tm = min(TM_DENSE, seq)
    return pl.pallas_call(
        functools.partial(_inproj_kernel, scaled_groups=scaled_groups),
        out_shape=jax.ShapeDtypeStruct((batch, n_groups, seq, LANES), BF16),
        grid=(batch, seq // tm),
        in_specs=[
            pl.BlockSpec((1, tm, D_MODEL), lambda b, i: (b, i, 0)),
            _const_spec((1, D_MODEL)),
            _const_spec(w.shape),
        ],
        out_specs=pl.BlockSpec((1, n_groups, tm, LANES), lambda b, i: (b, 0, i, 0)),
        compiler_params=_params(2),
        name="in_proj",
    )(x, gain.reshape(1, D_MODEL), w)


def _split_halves(x, lo_mask):
    zero = jnp.zeros_like(x)
    return jnp.concatenate([jnp.where(lo_mask, x, zero), jnp.where(lo_mask, zero, x)], axis=0)


def _pv_with_rowsum(e_bf16, v):
    v_aug = jnp.concatenate([v, jnp.ones_like(v)], axis=1)
    o_aug = jnp.dot(e_bf16, v_aug, preferred_element_type=F32)
    return o_aug[:, :LANES], o_aug[:, LANES:]


def _memory_heads(qm_ref, mk_ref, mv_ref, o_ref, lo_mask):
    tq = qm_ref.shape[2]
    for grp in range(MEM_GROUPS):
        qg = qm_ref[0, grp]
        zero = jnp.zeros_like(qg)
        lhs = jnp.concatenate([jnp.where(lo_mask, qg, zero), jnp.where(lo_mask, zero, qg)], axis=0)
        s = _nt_dot(lhs, mk_ref[0, grp])
        m = jnp.max(s, axis=-1, keepdims=True)
        e = jnp.exp2(s - m).astype(BF16)
        o, l = _pv_with_rowsum(e, mv_ref[0, grp])
        o = o / l
        o_ref[0, TOK_GROUPS + grp] = jnp.where(lo_mask, o[:tq], o[tq:]).astype(BF16)


def _t5_bucket(rel):
    half = N_BUCKETS // 2
    max_exact = half // 2
    n = -rel
    ret = jnp.where(n < 0, half, 0)
    n = jnp.abs(n)
    nf = jnp.maximum(n, 1).astype(jnp.float32)
    large = max_exact + (jnp.log(nf / max_exact) / math.log(MAX_DISTANCE / max_exact)
                         * (half - max_exact)).astype(jnp.int32)
    large = jnp.minimum(large, half - 1)
    return ret + jnp.where(n < max_exact, n, large)


def _bias_a_kernel(t5_ref, bucket_ref, o_ref):
    q = lax.broadcasted_iota(jnp.int32, (BLOCK, BLOCK), 0)
    j = lax.broadcasted_iota(jnp.int32, (BLOCK, BLOCK), 1)
    masked = jnp.full((BLOCK, BLOCK), NEG_INF, F32)
    valid = (j >= q, None, j <= q)
    for g in range(A_KV_HEADS):
        for t in range(A_GROUP):
            head = A_GROUP * g + t
            rows = slice(BLOCK * t, BLOCK * (t + 1))
            for kb, variant in ((0, 0), (1, 2), (2, 3)):
                bucket = bucket_ref[kb]
                acc = jnp.zeros((BLOCK, BLOCK), F32)
                for b in range(N_BUCKETS):
                    acc = jnp.where(bucket == b, t5_ref[b * A_Q_HEADS + head], acc)
                acc = acc * LOG2E
                if valid[kb] is not None:
                    acc = jnp.where(valid[kb], acc, masked)
                o_ref[variant, g, rows, :] = acc
            o_ref[1, g, rows, :] = masked
            o_ref[4, g, rows, :] = masked


def _bias_a(t5_table):
    q = jnp.arange(BLOCK)[:, None]
    j = jnp.arange(BLOCK)[None, :]
    rel = jnp.stack([j - WINDOW - q, j - q, j + WINDOW - q])
    bucket = _t5_bucket(rel).astype(jnp.int32)
    return pl.pallas_call(
        _bias_a_kernel,
        out_shape=jax.ShapeDtypeStruct((5, A_KV_HEADS, A_GROUP * BLOCK, BLOCK), F32),
        in_specs=[pl.BlockSpec(memory_space=pltpu.SMEM), pl.BlockSpec(memory_space=pltpu.VMEM)],
        out_specs=pl.BlockSpec(memory_space=pltpu.VMEM),
        name="bias_a",
    )(t5_table.reshape(-1), bucket)


def _attn_a_kernel(q_ref, kp_ref, kc_ref, kn_ref, vp_ref, vc_ref, vn_ref, qm_ref, mk_ref, mv_ref, tb_ref,
                   sink_ref, o_ref):
    n_sub = q_ref.shape[2] // BLOCK
    i = pl.program_id(1)
    last = pl.num_programs(1) - 1
    first_prev_v = jnp.where(i > 0, 0, 1)
    last_next_v = jnp.where(i < last, 3, 4)
    lo_mask = lax.broadcasted_iota(jnp.int32, (1, LANES), 1) < HEAD_DIM
    n_keys = 3 * BLOCK
    lo_ones = jnp.where(lax.broadcasted_iota(jnp.int32, (n_keys, LANES), 1) < HEAD_DIM, 1.0, 0.0)
    ones_halves = jnp.concatenate([lo_ones, 1.0 - lo_ones], axis=0).astype(BF16)
    lo_f32 = lax.broadcasted_iota(jnp.int32, (A_GROUP * BLOCK, LANES), 1) < HEAD_DIM

    def key_blocks(prev_ref, cur_ref, next_ref, p):
        blocks = [prev_ref[0, p]] + [cur_ref[0, p, BLOCK * s:BLOCK * (s + 1), :] for s in range(n_sub)]
        blocks.append(next_ref[0, p])
        zero = jnp.zeros_like(blocks[0])
        return ([jnp.where(lo_mask, blk, zero) for blk in blocks], [jnp.where(lo_mask, zero, blk) for blk in blocks])

    def window(halves, s):
        return jnp.concatenate(halves[0][s:s + 3] + halves[1][s:s + 3], axis=0)

    k_halves = [key_blocks(kp_ref, kc_ref, kn_ref, p) for p in range(A_KV_HEADS // 2)]
    v_halves = [key_blocks(vp_ref, vc_ref, vn_ref, p) for p in range(A_KV_HEADS // 2)]

    def qk(p, s):
        rows = slice(BLOCK * s, BLOCK * (s + 1))
        lhs3 = jnp.concatenate([q_ref[0, A_GROUP * p + t, rows, :] for t in range(A_GROUP)], axis=0)
        return _nt_dot(lhs3, window(k_halves[p], s))

    def finish(p, s, scores):
        variants = (first_prev_v if s == 0 else 0, 2, last_next_v if s == n_sub - 1 else 3)
        es, sink_terms = [], []
        for hh in range(2):
            g = 2 * p + hh
            sc = [scores[:, n_keys * hh + LANES * j:n_keys * hh + LANES * (j + 1)] + tb_ref[variants[j], g]
                  for j in range(3)]
            sink = sink_ref[g]
            m = jnp.maximum(jnp.max(jnp.maximum(jnp.maximum(sc[0], sc[1]), sc[2]), axis=-1, keepdims=True), sink)
            es += [jnp.exp2(sj - m).astype(BF16) for sj in sc]
            sink_terms.append(jnp.exp2(sink - m))
        v_aug = jnp.concatenate([window(v_halves[p], s), ones_halves], axis=1)
        o_aug = jnp.dot(jnp.concatenate(es, axis=1), v_aug, preferred_element_type=F32)
        l = o_aug[:, LANES:] + jnp.where(lo_f32, sink_terms[0], sink_terms[1])
        o3 = o_aug[:, :LANES] / l
        for t in range(A_GROUP):
            o_ref[0, A_GROUP * p + t, BLOCK * s:BLOCK * (s + 1), :] = o3[BLOCK * t:BLOCK * (t + 1)].astype(BF16)

    for s in range(n_sub):
        for p in range(A_KV_HEADS // 2):
            finish(p, s, qk(p, s))
    _memory_heads(qm_ref, mk_ref, mv_ref, o_ref, lo_mask)


def _attn_a(proj, memkv, tb, sink_tab):
    batch, _, seq, _ = proj.shape
    tq = min(A_TQ, seq)
    n_sub = tq // BLOCK
    n_steps = seq // tq
    nblk = seq // BLOCK
    assert nblk >= 2

    def halo(group_block, first):
        idx = (lambda i: jnp.maximum(n_sub * i - 1, 0)) if first else (lambda i: jnp.minimum(n_sub * (i + 1), nblk - 1))
        return pl.BlockSpec((1, 2, BLOCK, LANES), lambda b, i: (b, group_block, idx(i), 0))

    def cur(group_block):
        return pl.BlockSpec((1, 2, tq, LANES), lambda b, i: (b, group_block, i, 0))

    return pl.pallas_call(
        _attn_a_kernel,
        out_shape=jax.ShapeDtypeStruct((batch, CAT_GROUPS, seq, LANES), BF16),
        grid=(batch, n_steps),
        in_specs=[
            pl.BlockSpec((1, TOK_GROUPS, tq, LANES), lambda b, i: (b, 0, i, 0)),
            halo(3, True), cur(3), halo(3, False),
            halo(4, True), cur(4), halo(4, False),
            pl.BlockSpec((1, MEM_GROUPS, tq, LANES), lambda b, i: (b, 5, i, 0)),
            pl.BlockSpec((1, MEM_GROUPS, MEM_LEN, LANES), lambda b, i: (b, 0, 0, 0)),
            pl.BlockSpec((1, MEM_GROUPS, MEM_LEN, LANES), lambda b, i: (b, 1, 0, 0)),
            _const_spec(tb.shape),
            _const_spec(sink_tab.shape),
        ],
        out_specs=pl.BlockSpec((1, CAT_GROUPS, tq, LANES), lambda b, i: (b, 0, i, 0)),
        compiler_params=_params(2),
        name="attn_a",
    )(proj, proj, proj, proj, proj, proj, proj, proj, memkv, memkv, tb, sink_tab)


def _bias_b_kernel(rpb_ref, o_ref):
    h = pl.program_id(0)
    n_ri = 2 * NB_ROWS - 1
    n_ci = 2 * NB_COLS - 1
    int_lo = NB_ROWS // 2 - 1
    int_hi = int_lo + NB_ROWS
    c = lax.broadcasted_iota(jnp.int32, (GRID_W, LANES), 0)
    lane = lax.broadcasted_iota(jnp.int32, (GRID_W, LANES), 1)
    cc = jnp.bitwise_and(lane, GRID_W - 1)
    hi_half = lane >= GRID_W
    cs = jnp.clip(c - NB_COLS // 2, 0, GRID_W - NB_COLS)
    col_valid = (cc >= cs) & (cc < cs + NB_COLS)
    dd = cc - c + NB_COLS - 1

    def body(k, carry):
        ri_l = k - 4
        ri_r = k - 3
        ok_l = (ri_l >= 0) & (ri_l < n_ri)
        ok_r = (ri_r >= 0) & (ri_r < n_ri)
        base_l = (h * n_ri + jnp.clip(ri_l, 0, n_ri - 1)) * n_ci
        base_r = (h * n_ri + jnp.clip(ri_r, 0, n_ri - 1)) * n_ci
        acc = jnp.zeros((GRID_W, LANES), F32)
        for d in range(n_ci):
            v_l = jnp.where(ok_l, rpb_ref[base_l + d], 0.0)
            v_r = jnp.where(ok_r, rpb_ref[base_r + d], 0.0)
            acc = jnp.where(dd == d, jnp.where(hi_half, v_r, v_l), acc)
        tile = jnp.where(col_valid, acc * LOG2E, NEG_INF)
        o_ref[0, 0, k] = tile
        interior_l = (ri_l >= int_lo) & (ri_l < int_hi)
        interior_r = (ri_r >= int_lo) & (ri_r < int_hi)
        o_ref[1, 0, k] = jnp.where(hi_half, jnp.where(interior_r, tile, NEG_INF), jnp.where(interior_l, tile, NEG_INF))
        return carry

    lax.fori_loop(0, B_NK, body, 0)


def _bias_b(rpb):
    return pl.pallas_call(
        _bias_b_kernel,
        out_shape=jax.ShapeDtypeStruct((2, B_HEADS, B_NK, GRID_W, LANES), F32),
        grid=(B_HEADS,),
        in_specs=[pl.BlockSpec(memory_space=pltpu.SMEM)],
        out_specs=pl.BlockSpec((2, 1, B_NK, GRID_W, LANES), lambda h: (0, h, 0, 0, 0)),
        compiler_params=_params(1),
        name="bias_b",
    )(rpb.reshape(-1))


def _attn_b_kernel(q_ref, kp_ref, kc_ref, kn_ref, vp_ref, vc_ref, vn_ref, qm_ref, mk_ref, mv_ref, pv_ref, o_ref, *,
                   rows):
    i = pl.program_id(1)
    n_steps = pl.num_programs(1)
    lo_mask = lax.broadcasted_iota(jnp.int32, (1, LANES), 1) < HEAD_DIM
    n_keys = B_KEY_ROWS * GRID_W
    n_cols = B_KEY_ROWS // 2

    def scores_and_values(grp):
        qg = q_ref[0, grp]
        zero = jnp.zeros_like(qg)
        lhs = jnp.concatenate([jnp.where(lo_mask, qg, zero), jnp.where(lo_mask, zero, qg)], axis=0)
        kw = jnp.concatenate([kp_ref[0, grp], kc_ref[0, grp], kn_ref[0, grp]], axis=0)
        vw = jnp.concatenate([vp_ref[0, grp], vc_ref[0, grp], vn_ref[0, grp]], axis=0)
        return _nt_dot(lhs, kw), vw

    def finish(grp, e, vw):
        o, l = _pv_with_rowsum(e, vw)
        o = o / l
        o_ref[0, grp] = jnp.where(lo_mask, o[:B_TOK], o[B_TOK:]).astype(BF16)

    interior = (i >= 1) & (i <= n_steps - 2)

    @pl.when(interior)
    def _():
        for grp in range(TOK_GROUPS):
            s, vw = scores_and_values(grp)
            blocks = []
            for hh in range(2):
                for rl in range(B_ROWS_PER_STEP):
                    t0, t1 = rl // 2, (rl + NB_ROWS - 1) // 2 + 1
                    r0 = hh * B_TOK + rl * GRID_W
                    bias = jnp.concatenate([pv_ref[1, 2 * grp + hh, 2 * t - rl + NB_ROWS - 1] for t in range(t0, t1)],
                                           axis=1)
                    sb = s[r0:r0 + GRID_W, LANES * t0:LANES * t1] + bias
                    e = jnp.exp2(sb - jnp.max(sb, axis=-1, keepdims=True)).astype(BF16)
                    parts = [jnp.zeros((GRID_W, LANES * t0), BF16)] if t0 else []
                    parts.append(e)
                    if t1 < n_cols:
                        parts.append(jnp.zeros((GRID_W, LANES * (n_cols - t1)), BF16))
                    blocks.append(jnp.concatenate(parts, axis=1))
            finish(grp, jnp.concatenate(blocks, axis=0), vw)
        _memory_heads(qm_ref, mk_ref, mv_ref, o_ref, lo_mask)

    @pl.when(jnp.logical_not(interior))
    def _():
        ws = B_ROWS_PER_STEP * jnp.clip(i - 1, 0, n_steps - 3)
        key_row = lax.broadcasted_iota(jnp.int32, (1, n_keys), 1) // GRID_W
        table_k0 = []
        row_pen = []
        for rl in range(B_ROWS_PER_STEP):
            r = B_ROWS_PER_STEP * i + rl
            rs = jnp.clip(r - NB_ROWS // 2, 0, rows - NB_ROWS)
            kr0 = rs - ws
            table_k0.append(rs - r - kr0 + NB_ROWS - 1 + 4)
            row_pen.append(jnp.where((key_row >= kr0) & (key_row < kr0 + NB_ROWS), 0.0, NEG_INF).astype(F32))
        for grp in range(TOK_GROUPS):
            s, vw = scores_and_values(grp)
            biases = []
            for hh in range(2):
                for rl in range(B_ROWS_PER_STEP):
                    b = jnp.concatenate([pv_ref[0, 2 * grp + hh, table_k0[rl] + 2 * t] for t in range(n_cols)], axis=1)
                    biases.append(b + row_pen[rl])
            s = s + jnp.concatenate(biases, axis=0)
            m = jnp.max(s, axis=-1, keepdims=True)
            finish(grp, jnp.exp2(s - m).astype(BF16), vw)
        _memory_heads(qm_ref, mk_ref, mv_ref, o_ref, lo_mask)


def _attn_b(proj, memkv, pv):
    batch, _, seq, _ = proj.shape
    rows = seq // GRID_W
    n_steps = rows // B_ROWS_PER_STEP
    assert n_steps >= 3 and rows >= NB_ROWS
    kv_blk = (1, TOK_GROUPS, B_TOK, LANES)

    def halo(group_block, pos):
        return pl.BlockSpec(kv_blk, lambda b, i: (b, group_block, jnp.clip(i - 1, 0, n_steps - 3) + pos, 0))

    return pl.pallas_call(
        functools.partial(_attn_b_kernel, rows=rows),
        out_shape=jax.ShapeDtypeStruct((batch, CAT_GROUPS, seq, LANES), BF16),
        grid=(batch, n_steps),
        in_specs=[
            pl.BlockSpec((1, TOK_GROUPS, B_TOK, LANES), lambda b, i: (b, 0, i, 0)),
            halo(1, 0), halo(1, 1), halo(1, 2),
            halo(2, 0), halo(2, 1), halo(2, 2),
            pl.BlockSpec((1, MEM_GROUPS, B_TOK, LANES), lambda b, i: (b, 3 * TOK_GROUPS // MEM_GROUPS, i, 0)),
            pl.BlockSpec((1, MEM_GROUPS, MEM_LEN, LANES), lambda b, i: (b, 0, 0, 0)),
            pl.BlockSpec((1, MEM_GROUPS, MEM_LEN, LANES), lambda b, i: (b, 1, 0, 0)),
            _const_spec(pv.shape),
        ],
        out_specs=pl.BlockSpec((1, CAT_GROUPS, B_TOK, LANES), lambda b, i: (b, 0, i, 0)),
        compiler_params=_params(2),
        name="attn_b",
    )(proj, proj, proj, proj, proj, proj, proj, proj, memkv, memkv, pv)


def _post_kernel(x_ref, cat_ref, wo_ref, wgu_ref, wd_ref, gpost_ref, gpre_ref, gfpost_ref, o_ref):
    chain_rows = x_ref.shape[1] // POST_CHAINS
    chains = [slice(ch * chain_rows, (ch + 1) * chain_rows) for ch in range(POST_CHAINS)]

    def out_proj(rows):
        cat = jnp.concatenate([cat_ref[0, g, rows, :] for g in range(CAT_GROUPS)], axis=1)
        return jnp.dot(cat, wo_ref[...], preferred_element_type=F32)

    def ffn(h2):
        acc = jnp.zeros(h2.shape, F32)
        c0 = 0
        for size in FF_CHUNKS:
            cols = slice(c0, c0 + size)
            gate = jnp.dot(h2, wgu_ref[0, :, cols], preferred_element_type=F32)
            up = jnp.dot(h2, wgu_ref[1, :, cols], preferred_element_type=F32)
            act = (gate * jax.nn.sigmoid(gate) * up).astype(BF16)
            acc = acc + jnp.dot(act, wd_ref[cols, :], preferred_element_type=F32)
            c0 += size
        return acc

    mixed = [out_proj(rows) for rows in chains]
    for rows, mx in zip(chains, mixed):
        x1 = x_ref[0, rows, :] + _rms(mx, gpost_ref[...])
        h2 = _rms(x1, gpre_ref[...]).astype(BF16)
        o_ref[0, rows, :] = x1 + _rms(ffn(h2), gfpost_ref[...])


def _post(x, cat, w_out, w_gu, w_down, g_post, g_pre, g_fpost):
    batch, seq, _ = x.shape
    tm = min(TM_DENSE, seq)
    gain = lambda g: g.reshape(1, D_MODEL)
    x_spec = pl.BlockSpec((1, tm, D_MODEL), lambda b, i: (b, i, 0))
    return pl.pallas_call(
        _post_kernel,
        out_shape=jax.ShapeDtypeStruct(x.shape, F32),
        grid=(batch, seq // tm),
        in_specs=[
            x_spec,
            pl.BlockSpec((1, CAT_GROUPS, tm, LANES), lambda b, i: (b, 0, i, 0)),
            _const_spec(w_out.shape),
            _const_spec(w_gu.shape),
            _const_spec(w_down.shape),
            _const_spec((1, D_MODEL)),
            _const_spec((1, D_MODEL)),
            _const_spec((1, D_MODEL)),
        ],
        out_specs=x_spec,
        compiler_params=_params(2),
        name="post_ffn",
    )(x, cat, w_out, w_gu, w_down, gain(g_post), gain(g_pre), gain(g_fpost))


def _head_cols(order):
    return np.concatenate([np.arange(HEAD_DIM) + HEAD_DIM * h for h in order])


def _prep_w_in_a(w):
    q_cols = _head_cols(A_HEAD_ORDER)
    cols = np.concatenate([q_cols, np.arange(A_Q_HEADS * HEAD_DIM, w.shape[1])])
    return w[:, cols].astype(BF16)


def _prep_w_out_a(w):
    rows = np.concatenate([_head_cols(A_HEAD_ORDER), np.arange(A_Q_HEADS * HEAD_DIM, w.shape[0])])
    return w[rows, :].astype(BF16)


def _prep_w_gu(w):
    return jnp.stack([w[:, :D_FF], w[:, D_FF:]]).astype(BF16)


def kernel(x, mem, w_in_a, sink_a, w_in_b, rpb_b, t5_table, w_mem_kv, w_out, w_gu, w_down, norm_mix_pre,
           norm_mix_post, norm_mem, norm_ffn_pre, norm_ffn_post):
    depth = w_out.shape[0]
    memkv = _memkv(mem, norm_mem, w_mem_kv.astype(BF16))
    tb = _bias_a(t5_table)
    a_scaled = frozenset(range(TOK_GROUPS)) | frozenset(range(A_GROUPS - MEM_GROUPS, A_GROUPS))
    b_scaled = frozenset(range(TOK_GROUPS)) | frozenset(range(B_GROUPS - MEM_GROUPS, B_GROUPS))
    for i in range(depth):
        j = i // 2
        if i % 2 == 0:
            proj = _inproj(x, norm_mix_pre[i], _prep_w_in_a(w_in_a[j]), a_scaled)
            sink_tab = jnp.broadcast_to((sink_a[j].astype(F32) * LOG2E).reshape(A_KV_HEADS, A_GROUP, 1, 1),
                                        (A_KV_HEADS, A_GROUP, BLOCK, LANES)).reshape(A_KV_HEADS, A_GROUP * BLOCK, LANES)
            cat = _attn_a(proj, memkv[i], tb, sink_tab)
            wo = _prep_w_out_a(w_out[i])
        else:
            proj = _inproj(x, norm_mix_pre[i], w_in_b[j].astype(BF16), b_scaled)
            cat = _attn_b(proj, memkv[i], _bias_b(rpb_b[j]))
            wo = w_out[i].astype(BF16)
        x = _post(x, cat, wo, _prep_w_gu(w_gu[i]), w_down[i].astype(BF16),
                  norm_mix_post[i], norm_ffn_pre[i], norm_ffn_post[i])
    return x
```

```python
import functools
import math

import jax
import jax.numpy as jnp
import numpy as np
from jax import lax
from jax.experimental import pallas as pl
from jax.experimental.pallas import tpu as pltpu

F32 = jnp.float32
BF16 = jnp.bfloat16

D_MODEL = 1024
HEAD_DIM = 64
LANES = 128
A_Q_HEADS = 12
A_KV_HEADS = 4
A_GROUP = A_Q_HEADS // A_KV_HEADS
WINDOW = 128
BLOCK = 128
B_HEADS = 12
GRID_W = 64
NB_ROWS = 8
NB_COLS = 16
MEM_HEADS = 4
MEM_LEN = 256
N_BUCKETS = 32
MAX_DISTANCE = 128
D_FF = 2816
RMS_EPS = 1e-6
LOG2E = math.log2(math.e)
Q_SCALE = HEAD_DIM ** -0.5 * LOG2E

A_GROUPS = (A_Q_HEADS + 2 * A_KV_HEADS + MEM_HEADS) * HEAD_DIM // LANES
B_GROUPS = (3 * B_HEADS + MEM_HEADS) * HEAD_DIM // LANES
CAT_GROUPS = (A_Q_HEADS + MEM_HEADS) * HEAD_DIM // LANES
TOK_GROUPS = A_Q_HEADS * HEAD_DIM // LANES
MEM_GROUPS = MEM_HEADS * HEAD_DIM // LANES

A_HEAD_ORDER = tuple(3 * (2 * p + hh) + t for p in range(2) for t in range(3) for hh in range(2))

TM_DENSE = 1024
POST_CHAINS = 2
FF_CHUNKS = (256,) * 11
assert sum(FF_CHUNKS) == D_FF
A_TQ = 2048
B_ROWS_PER_STEP = 4
B_TOK = B_ROWS_PER_STEP * GRID_W
B_KEY_ROWS = 3 * B_ROWS_PER_STEP
B_NK = 22
NEG_INF = float("-inf")
VMEM_LIMIT = 56 * 1024 * 1024


def _rms(x, g):
    y = x * lax.rsqrt(jnp.mean(x * x, axis=-1, keepdims=True) + RMS_EPS)
    return y * g


def _nt_dot(a, b):
    return lax.dot_general(a, b, (((1,), (1,)), ((), ())), preferred_element_type=F32)


def _const_spec(shape):
    n = len(shape)
    return pl.BlockSpec(shape, lambda *_: (0,) * n, pipeline_mode=pl.Buffered(1))


def _params(n_axes, flags=None):
    return pltpu.CompilerParams(dimension_semantics=("arbitrary",) * n_axes, vmem_limit_bytes=VMEM_LIMIT,
                                flags=flags)


def _memkv_kernel(mem_ref, g_ref, w_ref, o_ref):
    batch = mem_ref.shape[0]
    m = _rms(mem_ref[...].reshape(batch * MEM_LEN, D_MODEL), g_ref[0]).astype(BF16)
    res = jnp.dot(m, w_ref[0], preferred_element_type=F32)
    for b in range(batch):
        for g in range(2 * MEM_GROUPS):
            o_ref[0, b, g] = res[b * MEM_LEN:(b + 1) * MEM_LEN, g * LANES:(g + 1) * LANES].astype(BF16)


def _memkv(mem, norm_mem, w_mem_kv):
    depth, batch = w_mem_kv.shape[0], mem.shape[0]
    return pl.pallas_call(
        _memkv_kernel,
        out_shape=jax.ShapeDtypeStruct((depth, batch, 2 * MEM_GROUPS, MEM_LEN, LANES), BF16),
        grid=(depth,),
        in_specs=[
            _const_spec(mem.shape),
            pl.BlockSpec((1, 1, D_MODEL), lambda l: (l, 0, 0)),
            pl.BlockSpec((1, D_MODEL, 2 * MEM_GROUPS * LANES), lambda l: (l, 0, 0)),
        ],
        out_specs=pl.BlockSpec((1, batch, 2 * MEM_GROUPS, MEM_LEN, LANES), lambda l: (l, 0, 0, 0, 0)),
        compiler_params=_params(1),
        name="mem_kv",
    )(mem, norm_mem.reshape(depth, 1, D_MODEL), w_mem_kv)


def _store_groups(o_ref, rows, res, scaled_groups):
    for g in range(o_ref.shape[1]):
        blk = res[:, g * LANES:(g + 1) * LANES]
        if g in scaled_groups:
            blk = blk * Q_SCALE
        o_ref[0, g, rows, :] = blk.astype(BF16)


def _inproj_kernel(x_ref, g_ref, w_ref, o_ref, *, scaled_groups):
    chain_rows = x_ref.shape[1] // POST_CHAINS
    for ch in range(POST_CHAINS):
        rows = slice(ch * chain_rows, (ch + 1) * chain_rows)
        h = _rms(x_ref[0, rows, :], g_ref[...]).astype(BF16)
        _store_groups(o_ref, rows, jnp.dot(h, w_ref[...], preferred_element_type=F32), scaled_groups)


def _inproj(x, gain, w, scaled_groups):
    batch, seq, _ = x.shape
    n_groups = w.shape[1] // LANES
    tm = min(TM_DENSE, seq)
    return pl.pallas_call(
        functools.partial(_inproj_kernel, scaled_groups=scaled_groups),
        out_shape=jax.ShapeDtypeStruct((batch, n_groups, seq, LANES), BF16),
        grid=(batch, seq // tm),
        in_specs=[
            pl.BlockSpec((1, tm, D_MODEL), lambda b, i: (b, i, 0)),
            _const_spec((1, D_MODEL)),
            _const_spec(w.shape),
        ],
        out_specs=pl.BlockSpec((1, n_groups, tm, LANES), lambda b, i: (b, 0, i, 0)),
        compiler_params=_params(2),
        name="in_proj",
    )(x, gain.reshape(1, D_MODEL), w)


def _split_halves(x, lo_mask):
    zero = jnp.zeros_like(x)
    return jnp.concatenate([jnp.where(lo_mask, x, zero), jnp.where(lo_mask, zero, x)], axis=0)


def _pv_with_rowsum(e_bf16, v):
    v_aug = jnp.concatenate([v, jnp.ones_like(v)], axis=1)
    o_aug = jnp.dot(e_bf16, v_aug, preferred_element_type=F32)
    return o_aug[:, :LANES], o_aug[:, LANES:]


def _memory_heads(qm_ref, mk_ref, mv_ref, o_ref, lo_mask):
    tq = qm_ref.shape[2]
    for grp in range(MEM_GROUPS):
        qg = qm_ref[0, grp]
        zero = jnp.zeros_like(qg)
        lhs = jnp.concatenate([jnp.where(lo_mask, qg, zero), jnp.where(lo_mask, zero, qg)], axis=0)
        s = _nt_dot(lhs, mk_ref[0, grp])
        m = jnp.max(s, axis=-1, keepdims=True)
        e = jnp.exp2(s - m).astype(BF16)
        o, l = _pv_with_rowsum(e, mv_ref[0, grp])
        o = o / l
        o_ref[0, TOK_GROUPS + grp] = jnp.where(lo_mask, o[:tq], o[tq:]).astype(BF16)


def _t5_bucket(rel):
    half = N_BUCKETS // 2
    max_exact = half // 2
    n = -rel
    ret = jnp.where(n < 0, half, 0)
    n = jnp.abs(n)
    nf = jnp.maximum(n, 1).astype(jnp.float32)
    large = max_exact + (jnp.log(nf / max_exact) / math.log(MAX_DISTANCE / max_exact)
                         * (half - max_exact)).astype(jnp.int32)
    large = jnp.minimum(large, half - 1)
    return ret + jnp.where(n < max_exact, n, large)


def _bias_a_kernel(t5_ref, bucket_ref, o_ref):
    q = lax.broadcasted_iota(jnp.int32, (BLOCK, BLOCK), 0)
    j = lax.broadcasted_iota(jnp.int32, (BLOCK, BLOCK), 1)
    masked = jnp.full((BLOCK, BLOCK), NEG_INF, F32)
    valid = (j >= q, None, j <= q)
    for g in range(A_KV_HEADS):
        for t in range(A_GROUP):
            head = A_GROUP * g + t
            rows = slice(BLOCK * t, BLOCK * (t + 1))
            for kb, variant in ((0, 0), (1, 2), (2, 3)):
                bucket = bucket_ref[kb]
                acc = jnp.zeros((BLOCK, BLOCK), F32)
                for b in range(N_BUCKETS):
                    acc = jnp.where(bucket == b, t5_ref[b * A_Q_HEADS + head], acc)
                acc = acc * LOG2E
                if valid[kb] is not None:
                    acc = jnp.where(valid[kb], acc, masked)
                o_ref[variant, g, rows, :] = acc
            o_ref[1, g, rows, :] = masked
            o_ref[4, g, rows, :] = masked


def _bias_a(t5_table):
    q = jnp.arange(BLOCK)[:, None]
    j = jnp.arange(BLOCK)[None, :]
    rel = jnp.stack([j - WINDOW - q, j - q, j + WINDOW - q])
    bucket = _t5_bucket(rel).astype(jnp.int32)
    return pl.pallas_call(
        _bias_a_kernel,
        out_shape=jax.ShapeDtypeStruct((5, A_KV_HEADS, A_GROUP * BLOCK, BLOCK), F32),
        in_specs=[pl.BlockSpec(memory_space=pltpu.SMEM), pl.BlockSpec(memory_space=pltpu.VMEM)],
        out_specs=pl.BlockSpec(memory_space=pltpu.VMEM),
        name="bias_a",
    )(t5_table.reshape(-1), bucket)


def _attn_a_kernel(q_ref, kp_ref, kc_ref, kn_ref, vp_ref, vc_ref, vn_ref, qm_ref, mk_ref, mv_ref, tb_ref,
                   sink_ref, o_ref):
    n_sub = q_ref.shape[2] // BLOCK
    i = pl.program_id(1)
    last = pl.num_programs(1) - 1
    first_prev_v = jnp.where(i > 0, 0, 1)
    last_next_v = jnp.where(i < last, 3, 4)
    lo_mask = lax.broadcasted_iota(jnp.int32, (1, LANES), 1) < HEAD_DIM
    n_keys = 3 * BLOCK
    lo_ones = jnp.where(lax.broadcasted_iota(jnp.int32, (n_keys, LANES), 1) < HEAD_DIM, 1.0, 0.0)
    ones_halves = jnp.concatenate([lo_ones, 1.0 - lo_ones], axis=0).astype(BF16)
    lo_f32 = lax.broadcasted_iota(jnp.int32, (A_GROUP * BLOCK, LANES), 1) < HEAD_DIM

    def key_blocks(prev_ref, cur_ref, next_ref, p):
        blocks = [prev_ref[0, p]] + [cur_ref[0, p, BLOCK * s:BLOCK * (s + 1), :] for s in range(n_sub)]
        blocks.append(next_ref[0, p])
        zero = jnp.zeros_like(blocks[0])
        return ([jnp.where(lo_mask, blk, zero) for blk in blocks], [jnp.where(lo_mask, zero, blk) for blk in blocks])

    def window(halves, s):
        return jnp.concatenate(halves[0][s:s + 3] + halves[1][s:s + 3], axis=0)

    k_halves = [key_blocks(kp_ref, kc_ref, kn_ref, p) for p in range(A_KV_HEADS // 2)]
    v_halves = [key_blocks(vp_ref, vc_ref, vn_ref, p) for p in range(A_KV_HEADS // 2)]

    def qk(p, s):
        rows = slice(BLOCK * s, BLOCK * (s + 1))
        lhs3 = jnp.concatenate([q_ref[0, A_GROUP * p + t, rows, :] for t in range(A_GROUP)], axis=0)
        return _nt_dot(lhs3, window(k_halves[p], s))

    def finish(p, s, scores):
        variants = (first_prev_v if s == 0 else 0, 2, last_next_v if s == n_sub - 1 else 3)
        es, sink_terms = [], []
        for hh in range(2):
            g = 2 * p + hh
            sc = [scores[:, n_keys * hh + LANES * j:n_keys * hh + LANES * (j + 1)] + tb_ref[variants[j], g]
                  for j in range(3)]
            sink = sink_ref[g]
            m = jnp.maximum(jnp.max(jnp.maximum(jnp.maximum(sc[0], sc[1]), sc[2]), axis=-1, keepdims=True), sink)
            es += [jnp.exp2(sj - m).astype(BF16) for sj in sc]
            sink_terms.append(jnp.exp2(sink - m))
        v_aug = jnp.concatenate([window(v_halves[p], s), ones_halves], axis=1)
        o_aug = jnp.dot(jnp.concatenate(es, axis=1), v_aug, preferred_element_type=F32)
        l = o_aug[:, LANES:] + jnp.where(lo_f32, sink_terms[0], sink_terms[1])
        o3 = o_aug[:, :LANES] / l
        for t in range(A_GROUP):
            o_ref[0, A_GROUP * p + t, BLOCK * s:BLOCK * (s + 1), :] = o3[BLOCK * t:BLOCK * (t + 1)].astype(BF16)

    for s in range(n_sub):
        for p in range(A_KV_HEADS // 2):
            finish(p, s, qk(p, s))
    _memory_heads(qm_ref, mk_ref, mv_ref, o_ref, lo_mask)


def _attn_a(proj, memkv, tb, sink_tab):
    batch, _, seq, _ = proj.shape
    tq = min(A_TQ, seq)
    n_sub = tq // BLOCK
    n_steps = seq // tq
    nblk = seq // BLOCK
    assert nblk >= 2

    def halo(group_block, first):
        idx = (lambda i: jnp.maximum(n_sub * i - 1, 0)) if first else (lambda i: jnp.minimum(n_sub * (i + 1), nblk - 1))
        return pl.BlockSpec((1, 2, BLOCK, LANES), lambda b, i: (b, group_block, idx(i), 0))

    def cur(group_block):
        return pl.BlockSpec((1, 2, tq, LANES), lambda b, i: (b, group_block, i, 0))

    return pl.pallas_call(
        _attn_a_kernel,
        out_shape=jax.ShapeDtypeStruct((batch, CAT_GROUPS, seq, LANES), BF16),
        grid=(batch, n_steps),
        in_specs=[
            pl.BlockSpec((1, TOK_GROUPS, tq, LANES), lambda b, i: (b, 0, i, 0)),
            halo(3, True), cur(3), halo(3, False),
            halo(4, True), cur(4), halo(4, False),
            pl.BlockSpec((1, MEM_GROUPS, tq, LANES), lambda b, i: (b, 5, i, 0)),
            pl.BlockSpec((1, MEM_GROUPS, MEM_LEN, LANES), lambda b, i: (b, 0, 0, 0)),
            pl.BlockSpec((1, MEM_GROUPS, MEM_LEN, LANES), lambda b, i: (b, 1, 0, 0)),
            _const_spec(tb.shape),
            _const_spec(sink_tab.shape),
        ],
        out_specs=pl.BlockSpec((1, CAT_GROUPS, tq, LANES), lambda b, i: (b, 0, i, 0)),
        compiler_params=_params(2),
        name="attn_a",
    )(proj, proj, proj, proj, proj, proj, proj, proj, memkv, memkv, tb, sink_tab)


def _bias_b_kernel(rpb_ref, o_ref):
    h = pl.program_id(0)
    n_ri = 2 * NB_ROWS - 1
    n_ci = 2 * NB_COLS - 1
    int_lo = NB_ROWS // 2 - 1
    int_hi = int_lo + NB_ROWS
    c = lax.broadcasted_iota(jnp.int32, (GRID_W, LANES), 0)
    lane = lax.broadcasted_iota(jnp.int32, (GRID_W, LANES), 1)
    cc = jnp.bitwise_and(lane, GRID_W - 1)
    hi_half = lane >= GRID_W
    cs = jnp.clip(c - NB_COLS // 2, 0, GRID_W - NB_COLS)
    col_valid = (cc >= cs) & (cc < cs + NB_COLS)
    dd = cc - c + NB_COLS - 1

    def body(k, carry):
        ri_l = k - 4
        ri_r = k - 3
        ok_l = (ri_l >= 0) & (ri_l < n_ri)
        ok_r = (ri_r >= 0) & (ri_r < n_ri)
        base_l = (h * n_ri + jnp.clip(ri_l, 0, n_ri - 1)) * n_ci
        base_r = (h * n_ri + jnp.clip(ri_r, 0, n_ri - 1)) * n_ci
        acc = jnp.zeros((GRID_W, LANES), F32)
        for d in range(n_ci):
            v_l = jnp.where(ok_l, rpb_ref[base_l + d], 0.0)
            v_r = jnp.where(ok_r, rpb_ref[base_r + d], 0.0)
            acc = jnp.where(dd == d, jnp.where(hi_half, v_r, v_l), acc)
        tile = jnp.where(col_valid, acc * LOG2E, NEG_INF)
        o_ref[0, 0, k] = tile
        interior_l = (ri_l >= int_lo) & (ri_l < int_hi)
        interior_r = (ri_r >= int_lo) & (ri_r < int_hi)
        o_ref[1, 0, k] = jnp.where(hi_half, jnp.where(interior_r, tile, NEG_INF), jnp.where(interior_l, tile, NEG_INF))
        return carry

    lax.fori_loop(0, B_NK, body, 0)


def _bias_b(rpb):
    return pl.pallas_call(
        _bias_b_kernel,
        out_shape=jax.ShapeDtypeStruct((2, B_HEADS, B_NK, GRID_W, LANES), F32),
        grid=(B_HEADS,),
        in_specs=[pl.BlockSpec(memory_space=pltpu.SMEM)],
        out_specs=pl.BlockSpec((2, 1, B_NK, GRID_W, LANES), lambda h: (0, h, 0, 0, 0)),
        compiler_params=_params(1),
        name="bias_b",
    )(rpb.reshape(-1))


def _attn_b_kernel(q_ref, kvp_ref, kvc_ref, kvn_ref, qm_ref, mk_ref, mv_ref, pv_ref, o_ref, *, rows):
    i = pl.program_id(1)
    n_steps = pl.num_programs(1)
    lo_mask = lax.broadcasted_iota(jnp.int32, (1, LANES), 1) < HEAD_DIM
    n_keys = B_KEY_ROWS * GRID_W
    n_cols = B_KEY_ROWS // 2

    def scores_and_values(grp):
        qg = q_ref[0, grp]
        zero = jnp.zeros_like(qg)
        lhs = jnp.concatenate([jnp.where(lo_mask, qg, zero), jnp.where(lo_mask, zero, qg)], axis=0)
        kv_blocks = (kvp_ref, kvc_ref, kvn_ref)
        kw = jnp.concatenate([ref[0, grp] for ref in kv_blocks], axis=0)
        vw = jnp.concatenate([ref[0, TOK_GROUPS + grp] for ref in kv_blocks], axis=0)
        return _nt_dot(lhs, kw), vw

    def finish(grp, e, vw):
        o, l = _pv_with_rowsum(e, vw)
        o = o / l
        o_ref[0, grp] = jnp.where(lo_mask, o[:B_TOK], o[B_TOK:]).astype(BF16)

    interior = (i >= 1) & (i <= n_steps - 2)

    @pl.when(interior)
    def _():
        for grp in range(TOK_GROUPS):
            s, vw = scores_and_values(grp)
            blocks = []
            for hh in range(2):
                for rl in range(B_ROWS_PER_STEP):
                    t0, t1 = rl // 2, (rl + NB_ROWS - 1) // 2 + 1
                    r0 = hh * B_TOK + rl * GRID_W
                    bias = jnp.concatenate([pv_ref[1, 2 * grp + hh, 2 * t - rl + NB_ROWS - 1] for t in range(t0, t1)],
                                           axis=1)
                    sb = s[r0:r0 + GRID_W, LANES * t0:LANES * t1] + bias
                    e = jnp.exp2(sb - jnp.max(sb, axis=-1, keepdims=True)).astype(BF16)
                    parts = [jnp.zeros((GRID_W, LANES * t0), BF16)] if t0 else []
                    parts.append(e)
                    if t1 < n_cols:
                        parts.append(jnp.zeros((GRID_W, LANES * (n_cols - t1)), BF16))
                    blocks.append(jnp.concatenate(parts, axis=1))
            finish(grp, jnp.concatenate(blocks, axis=0), vw)
        _memory_heads(qm_ref, mk_ref, mv_ref, o_ref, lo_mask)

    @pl.when(jnp.logical_not(interior))
    def _():
        ws = B_ROWS_PER_STEP * jnp.clip(i - 1, 0, n_steps - 3)
        key_row = lax.broadcasted_iota(jnp.int32, (1, n_keys), 1) // GRID_W
        table_k0 = []
        row_pen = []
        for rl in range(B_ROWS_PER_STEP):
            r = B_ROWS_PER_STEP * i + rl
            rs = jnp.clip(r - NB_ROWS // 2, 0, rows - NB_ROWS)
            kr0 = rs - ws
            table_k0.append(rs - r - kr0 + NB_ROWS - 1 + 4)
            row_pen.append(jnp.where((key_row >= kr0) & (key_row < kr0 + NB_ROWS), 0.0, NEG_INF).astype(F32))
        for grp in range(TOK_GROUPS):
            s, vw = scores_and_values(grp)
            biases = []
            for hh in range(2):
                for rl in range(B_ROWS_PER_STEP):
                    b = jnp.concatenate([pv_ref[0, 2 * grp + hh, table_k0[rl] + 2 * t] for t in range(n_cols)], axis=1)
                    biases.append(b + row_pen[rl])
            s = s + jnp.concatenate(biases, axis=0)
            m = jnp.max(s, axis=-1, keepdims=True)
            finish(grp, jnp.exp2(s - m).astype(BF16), vw)
        _memory_heads(qm_ref, mk_ref, mv_ref, o_ref, lo_mask)


def _attn_b(proj, memkv, pv):
    batch, _, seq, _ = proj.shape
    rows = seq // GRID_W
    n_steps = rows // B_ROWS_PER_STEP
    assert n_steps >= 3 and rows >= NB_ROWS
    kv_blk = (1, 2 * TOK_GROUPS, B_TOK, LANES)

    def halo(pos):
        return pl.BlockSpec(kv_blk, lambda b, i: (b, 0, jnp.clip(i - 1, 0, n_steps - 3) + pos, 0))

    return pl.pallas_call(
        functools.partial(_attn_b_kernel, rows=rows),
        out_shape=jax.ShapeDtypeStruct((batch, CAT_GROUPS, seq, LANES), BF16),
        grid=(batch, n_steps),
        in_specs=[
            pl.BlockSpec((1, TOK_GROUPS, B_TOK, LANES), lambda b, i: (b, 2, i, 0)),
            halo(0), halo(1), halo(2),
            pl.BlockSpec((1, MEM_GROUPS, B_TOK, LANES), lambda b, i: (b, 3 * TOK_GROUPS // MEM_GROUPS, i, 0)),
            pl.BlockSpec((1, MEM_GROUPS, MEM_LEN, LANES), lambda b, i: (b, 0, 0, 0)),
            pl.BlockSpec((1, MEM_GROUPS, MEM_LEN, LANES), lambda b, i: (b, 1, 0, 0)),
            _const_spec(pv.shape),
        ],
        out_specs=pl.BlockSpec((1, CAT_GROUPS, B_TOK, LANES), lambda b, i: (b, 0, i, 0)),
        compiler_params=_params(2),
        name="attn_b",
    )(proj, proj, proj, proj, proj, memkv, memkv, pv)


def _post_kernel(x_ref, cat_ref, wo_ref, wgu_ref, wd_ref, gpost_ref, gpre_ref, gfpost_ref, o_ref):
    chain_rows = x_ref.shape[1] // POST_CHAINS
    chains = [slice(ch * chain_rows, (ch + 1) * chain_rows) for ch in range(POST_CHAINS)]

    def out_proj(rows):
        cat = jnp.concatenate([cat_ref[0, g, rows, :] for g in range(CAT_GROUPS)], axis=1)
        return jnp.dot(cat, wo_ref[...], preferred_element_type=F32)

    def ffn(h2):
        acc = jnp.zeros(h2.shape, F32)
        c0 = 0
        for size in FF_CHUNKS:
            cols = slice(c0, c0 + size)
            gate = jnp.dot(h2, wgu_ref[0, :, cols], preferred_element_type=F32)
            up = jnp.dot(h2, wgu_ref[1, :, cols], preferred_element_type=F32)
            act = (gate * jax.nn.sigmoid(gate) * up).astype(BF16)
            acc = acc + jnp.dot(act, wd_ref[cols, :], preferred_element_type=F32)
            c0 += size
        return acc

    mixed = [out_proj(rows) for rows in chains]
    for rows, mx in zip(chains, mixed):
        x1 = x_ref[0, rows, :] + _rms(mx, gpost_ref[...])
        h2 = _rms(x1, gpre_ref[...]).astype(BF16)
        o_ref[0, rows, :] = x1 + _rms(ffn(h2), gfpost_ref[...])


def _post(x, cat, w_out, w_gu, w_down, g_post, g_pre, g_fpost):
    batch, seq, _ = x.shape
    tm = min(TM_DENSE, seq)
    gain = lambda g: g.reshape(1, D_MODEL)
    x_spec = pl.BlockSpec((1, tm, D_MODEL), lambda b, i: (b, i, 0))
    return pl.pallas_call(
        _post_kernel,
        out_shape=jax.ShapeDtypeStruct(x.shape, F32),
        grid=(batch, seq // tm),
        in_specs=[
            x_spec,
            pl.BlockSpec((1, CAT_GROUPS, tm, LANES), lambda b, i: (b, 0, i, 0)),
            _const_spec(w_out.shape),
            _const_spec(w_gu.shape),
            _const_spec(w_down.shape),
            _const_spec((1, D_MODEL)),
            _const_spec((1, D_MODEL)),
            _const_spec((1, D_MODEL)),
        ],
        out_specs=x_spec,
        compiler_params=_params(2),
        name="post_ffn",
    )(x, cat, w_out, w_gu, w_down, gain(g_post), gain(g_pre), gain(g_fpost))


def _head_cols(order):
    return np.concatenate([np.arange(HEAD_DIM) + HEAD_DIM * h for h in order])


def _prep_w_in_a(w):
    q_cols = _head_cols(A_HEAD_ORDER)
    cols = np.concatenate([q_cols, np.arange(A_Q_HEADS * HEAD_DIM, w.shape[1])])
    return w[:, cols].astype(BF16)


def _prep_w_in_b(w):
    q_width = B_HEADS * HEAD_DIM
    cols = np.concatenate([np.arange(q_width, 3 * q_width), np.arange(q_width), np.arange(3 * q_width, w.shape[1])])
    return w[:, cols].astype(BF16)


def _prep_w_out_a(w):
    rows = np.concatenate([_head_cols(A_HEAD_ORDER), np.arange(A_Q_HEADS * HEAD_DIM, w.shape[0])])
    return w[rows, :].astype(BF16)


def _prep_w_gu(w):
    return jnp.stack([w[:, :D_FF], w[:, D_FF:]]).astype(BF16)


def kernel(x, mem, w_in_a, sink_a, w_in_b, rpb_b, t5_table, w_mem_kv, w_out, w_gu, w_down, norm_mix_pre,
           norm_mix_post, norm_mem, norm_ffn_pre, norm_ffn_post):
    depth = w_out.shape[0]
    memkv = _memkv(mem, norm_mem, w_mem_kv.astype(BF16))
    tb = _bias_a(t5_table)
    a_scaled = frozenset(range(TOK_GROUPS)) | frozenset(range(A_GROUPS - MEM_GROUPS, A_GROUPS))
    b_scaled = frozenset(range(2 * TOK_GROUPS, B_GROUPS))
    for i in range(depth):
        j = i // 2
        if i % 2 == 0:
            proj = _inproj(x, norm_mix_pre[i], _prep_w_in_a(w_in_a[j]), a_scaled)
            sink_tab = jnp.broadcast_to((sink_a[j].astype(F32) * LOG2E).reshape(A_KV_HEADS, A_GROUP, 1, 1),
                                        (A_KV_HEADS, A_GROUP, BLOCK, LANES)).reshape(A_KV_HEADS, A_GROUP * BLOCK, LANES)
            cat = _attn_a(proj, memkv[i], tb, sink_tab)
            wo = _prep_w_out_a(w_out[i])
        else:
            proj = _inproj(x, norm_mix_pre[i], _prep_w_in_b(w_in_b[j]), b_scaled)
            cat = _attn_b(proj, memkv[i], _bias_b(rpb_b[j]))
            wo = w_out[i].astype(BF16)
        x = _post(x, cat, wo, _prep_w_gu(w_gu[i]), w_down[i].astype(BF16),
                  norm_mix_post[i], norm_ffn_pre[i], norm_ffn_post[i])
    return x
```

```python
import functools
import math

import jax
import jax.numpy as jnp
import numpy as np
from jax import lax
from jax.experimental import pallas as pl
from jax.experimental.pallas import tpu as pltpu

F32 = jnp.float32
BF16 = jnp.bfloat16

D_MODEL = 1024
HEAD_DIM = 64
LANES = 128
A_Q_HEADS = 12
A_KV_HEADS = 4
A_GROUP = A_Q_HEADS // A_KV_HEADS
WINDOW = 128
BLOCK = 128
B_HEADS = 12
GRID_W = 64
NB_ROWS = 8
NB_COLS = 16
MEM_HEADS = 4
MEM_LEN = 256
N_BUCKETS = 32
MAX_DISTANCE = 128
D_FF = 2816
RMS_EPS = 1e-6
LOG2E = math.log2(math.e)
Q_SCALE = HEAD_DIM ** -0.5 * LOG2E

A_GROUPS = (A_Q_HEADS + 2 * A_KV_HEADS + MEM_HEADS) * HEAD_DIM // LANES
B_GROUPS = (3 * B_HEADS + MEM_HEADS) * HEAD_DIM // LANES
CAT_GROUPS = (A_Q_HEADS + MEM_HEADS) * HEAD_DIM // LANES
TOK_GROUPS = A_Q_HEADS * HEAD_DIM // LANES
MEM_GROUPS = MEM_HEADS * HEAD_DIM // LANES

A_HEAD_ORDER = tuple(3 * (2 * p + hh) + t for p in range(2) for t in range(3) for hh in range(2))

TM_DENSE = 1024
POST_CHAINS = 2
FF_CHUNKS = (256,) * 11
assert sum(FF_CHUNKS) == D_FF
A_TQ = 2048
B_ROWS_PER_STEP = 4
B_TOK = B_ROWS_PER_STEP * GRID_W
B_SUBS = 2
B_KEY_ROWS = 3 * B_ROWS_PER_STEP
B_NK = 22
NEG_INF = float("-inf")
VMEM_LIMIT = 56 * 1024 * 1024


def _rms(x, g):
    y = x * lax.rsqrt(jnp.mean(x * x, axis=-1, keepdims=True) + RMS_EPS)
    return y * g


def _nt_dot(a, b):
    return lax.dot_general(a, b, (((1,), (1,)), ((), ())), preferred_element_type=F32)


def _const_spec(shape):
    n = len(shape)
    return pl.BlockSpec(shape, lambda *_: (0,) * n, pipeline_mode=pl.Buffered(1))


def _params(n_axes, flags=None):
    return pltpu.CompilerParams(dimension_semantics=("arbitrary",) * n_axes, vmem_limit_bytes=VMEM_LIMIT,
                                flags=flags)


def _memkv_kernel(mem_ref, g_ref, w_ref, o_ref):
    batch = mem_ref.shape[0]
    m = _rms(mem_ref[...].reshape(batch * MEM_LEN, D_MODEL), g_ref[0]).astype(BF16)
    res = jnp.dot(m, w_ref[0], preferred_element_type=F32)
    for b in range(batch):
        for g in range(2 * MEM_GROUPS):
            o_ref[0, b, g] = res[b * MEM_LEN:(b + 1) * MEM_LEN, g * LANES:(g + 1) * LANES].astype(BF16)


def _memkv(mem, norm_mem, w_mem_kv):
    depth, batch = w_mem_kv.shape[0], mem.shape[0]
    return pl.pallas_call(
        _memkv_kernel,
        out_shape=jax.ShapeDtypeStruct((depth, batch, 2 * MEM_GROUPS, MEM_LEN, LANES), BF16),
        grid=(depth,),
        in_specs=[
            _const_spec(mem.shape),
            pl.BlockSpec((1, 1, D_MODEL), lambda l: (l, 0, 0)),
            pl.BlockSpec((1, D_MODEL, 2 * MEM_GROUPS * LANES), lambda l: (l, 0, 0)),
        ],
        out_specs=pl.BlockSpec((1, batch, 2 * MEM_GROUPS, MEM_LEN, LANES), lambda l: (l, 0, 0, 0, 0)),
        compiler_params=_params(1),
        name="mem_kv",
    )(mem, norm_mem.reshape(depth, 1, D_MODEL), w_mem_kv)


def _store_groups(o_ref, rows, res, scaled_groups):
    for g in range(o_ref.shape[1]):
        blk = res[:, g * LANES:(g + 1) * LANES]
        if g in scaled_groups:
            blk = blk * Q_SCALE
        o_ref[0, g, rows, :] = blk.astype(BF16)


def _inproj_kernel(x_ref, g_ref, w_ref, o_ref, *, scaled_groups):
    chain_rows = x_ref.shape[1] // POST_CHAINS
    for ch in range(POST_CHAINS):
        rows = slice(ch * chain_rows, (ch + 1) * chain_rows)
        h = _rms(x_ref[0, rows, :], g_ref[...]).astype(BF16)
        _store_groups(o_ref, rows, jnp.dot(h, w_ref[...], preferred_element_type=F32), scaled_groups)


def _inproj(x, gain, w, scaled_groups):
    batch, seq, _ = x.shape
    n_groups = w.shape[1] // LANES
    tm = min(TM_DENSE, seq)
    return pl.pallas_call(
        functools.partial(_inproj_kernel, scaled_groups=scaled_groups),
        out_shape=jax.ShapeDtypeStruct((batch, n_groups, seq, LANES), BF16),
        grid=(batch, seq // tm),
        in_specs=[
            pl.BlockSpec((1, tm, D_MODEL), lambda b, i: (b, i, 0)),
            _const_spec((1, D_MODEL)),
            _const_spec(w.shape),
        ],
        out_specs=pl.BlockSpec((1, n_groups, tm, LANES), lambda b, i: (b, 0, i, 0)),
        compiler_params=_params(2),
        name="in_proj",
    )(x, gain.reshape(1, D_MODEL), w)


def _split_halves(x, lo_mask):
    zero = jnp.zeros_like(x)
    return jnp.concatenate([jnp.where(lo_mask, x, zero), jnp.where(lo_mask, zero, x)], axis=0)


def _pv_with_rowsum(e_bf16, v):
    v_aug = jnp.concatenate([v, jnp.ones_like(v)], axis=1)
    o_aug = jnp.dot(e_bf16, v_aug, preferred_element_type=F32)
    return o_aug[:, :LANES], o_aug[:, LANES:]


def _memory_heads(qm_ref, mk_ref, mv_ref, o_ref, lo_mask):
    tq = qm_ref.shape[2]
    for grp in range(MEM_GROUPS):
        qg = qm_ref[0, grp]
        zero = jnp.zeros_like(qg)
        lhs = jnp.concatenate([jnp.where(lo_mask, qg, zero), jnp.where(lo_mask, zero, qg)], axis=0)
        s = _nt_dot(lhs, mk_ref[0, grp])
        m = jnp.max(s, axis=-1, keepdims=True)
        e = jnp.exp2(s - m).astype(BF16)
        o, l = _pv_with_rowsum(e, mv_ref[0, grp])
        o = o / l
        o_ref[0, TOK_GROUPS + grp] = jnp.where(lo_mask, o[:tq], o[tq:]).astype(BF16)


def _t5_bucket(rel):
    half = N_BUCKETS // 2
    max_exact = half // 2
    n = -rel
    ret = jnp.where(n < 0, half, 0)
    n = jnp.abs(n)
    nf = jnp.maximum(n, 1).astype(jnp.float32)
    large = max_exact + (jnp.log(nf / max_exact) / math.log(MAX_DISTANCE / max_exact)
                         * (half - max_exact)).astype(jnp.int32)
    large = jnp.minimum(large, half - 1)
    return ret + jnp.where(n < max_exact, n, large)


def _bias_a_kernel(t5_ref, bucket_ref, o_ref):
    q = lax.broadcasted_iota(jnp.int32, (BLOCK, BLOCK), 0)
    j = lax.broadcasted_iota(jnp.int32, (BLOCK, BLOCK), 1)
    masked = jnp.full((BLOCK, BLOCK), NEG_INF, F32)
    valid = (j >= q, None, j <= q)
    for g in range(A_KV_HEADS):
        for t in range(A_GROUP):
            head = A_GROUP * g + t
            rows = slice(BLOCK * t, BLOCK * (t + 1))
            for kb, variant in ((0, 0), (1, 2), (2, 3)):
                bucket = bucket_ref[kb]
                acc = jnp.zeros((BLOCK, BLOCK), F32)
                for b in range(N_BUCKETS):
                    acc = jnp.where(bucket == b, t5_ref[b * A_Q_HEADS + head], acc)
                acc = acc * LOG2E
                if valid[kb] is not None:
                    acc = jnp.where(valid[kb], acc, masked)
                o_ref[variant, g, rows, :] = acc
            o_ref[1, g, rows, :] = masked
            o_ref[4, g, rows, :] = masked


def _bias_a(t5_table):
    q = jnp.arange(BLOCK)[:, None]
    j = jnp.arange(BLOCK)[None, :]
    rel = jnp.stack([j - WINDOW - q, j - q, j + WINDOW - q])
    bucket = _t5_bucket(rel).astype(jnp.int32)
    return pl.pallas_call(
        _bias_a_kernel,
        out_shape=jax.ShapeDtypeStruct((5, A_KV_HEADS, A_GROUP * BLOCK, BLOCK), F32),
        in_specs=[pl.BlockSpec(memory_space=pltpu.SMEM), pl.BlockSpec(memory_space=pltpu.VMEM)],
        out_specs=pl.BlockSpec(memory_space=pltpu.VMEM),
        name="bias_a",
    )(t5_table.reshape(-1), bucket)


def _attn_a_kernel(q_ref, kp_ref, kc_ref, kn_ref, vp_ref, vc_ref, vn_ref, qm_ref, mk_ref, mv_ref, tb_ref,
                   sink_ref, o_ref):
    n_sub = q_ref.shape[2] // BLOCK
    i = pl.program_id(1)
    last = pl.num_programs(1) - 1
    first_prev_v = jnp.where(i > 0, 0, 1)
    last_next_v = jnp.where(i < last, 3, 4)
    lo_mask = lax.broadcasted_iota(jnp.int32, (1, LANES), 1) < HEAD_DIM
    n_keys = 3 * BLOCK
    lo_ones = jnp.where(lax.broadcasted_iota(jnp.int32, (n_keys, LANES), 1) < HEAD_DIM, 1.0, 0.0)
    ones_halves = jnp.concatenate([lo_ones, 1.0 - lo_ones], axis=0).astype(BF16)
    lo_f32 = lax.broadcasted_iota(jnp.int32, (A_GROUP * BLOCK, LANES), 1) < HEAD_DIM

    def key_blocks(prev_ref, cur_ref, next_ref, p):
        blocks = [prev_ref[0, p]] + [cur_ref[0, p, BLOCK * s:BLOCK * (s + 1), :] for s in range(n_sub)]
        blocks.append(next_ref[0, p])
        zero = jnp.zeros_like(blocks[0])
        return ([jnp.where(lo_mask, blk, zero) for blk in blocks], [jnp.where(lo_mask, zero, blk) for blk in blocks])

    def window(halves, s):
        return jnp.concatenate(halves[0][s:s + 3] + halves[1][s:s + 3], axis=0)

    k_halves = [key_blocks(kp_ref, kc_ref, kn_ref, p) for p in range(A_KV_HEADS // 2)]
    v_halves = [key_blocks(vp_ref, vc_ref, vn_ref, p) for p in range(A_KV_HEADS // 2)]

    def qk(p, s):
        rows = slice(BLOCK * s, BLOCK * (s + 1))
        lhs3 = jnp.concatenate([q_ref[0, A_GROUP * p + t, rows, :] for t in range(A_GROUP)], axis=0)
        return _nt_dot(lhs3, window(k_halves[p], s))

    def finish(p, s, scores):
        variants = (first_prev_v if s == 0 else 0, 2, last_next_v if s == n_sub - 1 else 3)
        es, sink_terms = [], []
        for hh in range(2):
            g = 2 * p + hh
            sc = [scores[:, n_keys * hh + LANES * j:n_keys * hh + LANES * (j + 1)] + tb_ref[variants[j], g]
                  for j in range(3)]
            sink = sink_ref[g]
            m = jnp.maximum(jnp.max(jnp.maximum(jnp.maximum(sc[0], sc[1]), sc[2]), axis=-1, keepdims=True), sink)
            es += [jnp.exp2(sj - m).astype(BF16) for sj in sc]
            sink_terms.append(jnp.exp2(sink - m))
        v_aug = jnp.concatenate([window(v_halves[p], s), ones_halves], axis=1)
        o_aug = jnp.dot(jnp.concatenate(es, axis=1), v_aug, preferred_element_type=F32)
        l = o_aug[:, LANES:] + jnp.where(lo_f32, sink_terms[0], sink_terms[1])
        o3 = o_aug[:, :LANES] / l
        for t in range(A_GROUP):
            o_ref[0, A_GROUP * p + t, BLOCK * s:BLOCK * (s + 1), :] = o3[BLOCK * t:BLOCK * (t + 1)].astype(BF16)

    for s in range(n_sub):
        for p in range(A_KV_HEADS // 2):
            finish(p, s, qk(p, s))
    _memory_heads(qm_ref, mk_ref, mv_ref, o_ref, lo_mask)


def _attn_a(proj, memkv, tb, sink_tab):
    batch, _, seq, _ = proj.shape
    tq = min(A_TQ, seq)
    n_sub = tq // BLOCK
    n_steps = seq // tq
    nblk = seq // BLOCK
    assert nblk >= 2

    def halo(group_block, first):
        idx = (lambda i: jnp.maximum(n_sub * i - 1, 0)) if first else (lambda i: jnp.minimum(n_sub * (i + 1), nblk - 1))
        return pl.BlockSpec((1, 2, BLOCK, LANES), lambda b, i: (b, group_block, idx(i), 0))

    def cur(group_block):
        return pl.BlockSpec((1, 2, tq, LANES), lambda b, i: (b, group_block, i, 0))

    return pl.pallas_call(
        _attn_a_kernel,
        out_shape=jax.ShapeDtypeStruct((batch, CAT_GROUPS, seq, LANES), BF16),
        grid=(batch, n_steps),
        in_specs=[
            pl.BlockSpec((1, TOK_GROUPS, tq, LANES), lambda b, i: (b, 0, i, 0)),
            halo(3, True), cur(3), halo(3, False),
            halo(4, True), cur(4), halo(4, False),
            pl.BlockSpec((1, MEM_GROUPS, tq, LANES), lambda b, i: (b, 5, i, 0)),
            pl.BlockSpec((1, MEM_GROUPS, MEM_LEN, LANES), lambda b, i: (b, 0, 0, 0)),
            pl.BlockSpec((1, MEM_GROUPS, MEM_LEN, LANES), lambda b, i: (b, 1, 0, 0)),
            _const_spec(tb.shape),
            _const_spec(sink_tab.shape),
        ],
        out_specs=pl.BlockSpec((1, CAT_GROUPS, tq, LANES), lambda b, i: (b, 0, i, 0)),
        compiler_params=_params(2),
        name="attn_a",
    )(proj, proj, proj, proj, proj, proj, proj, proj, memkv, memkv, tb, sink_tab)


def _bias_b_kernel(rpb_ref, o_ref):
    h = pl.program_id(0)
    n_ri = 2 * NB_ROWS - 1
    n_ci = 2 * NB_COLS - 1
    int_lo = NB_ROWS // 2 - 1
    int_hi = int_lo + NB_ROWS
    c = lax.broadcasted_iota(jnp.int32, (GRID_W, LANES), 0)
    lane = lax.broadcasted_iota(jnp.int32, (GRID_W, LANES), 1)
    cc = jnp.bitwise_and(lane, GRID_W - 1)
    hi_half = lane >= GRID_W
    cs = jnp.clip(c - NB_COLS // 2, 0, GRID_W - NB_COLS)
    col_valid = (cc >= cs) & (cc < cs + NB_COLS)
    dd = cc - c + NB_COLS - 1

    def body(k, carry):
        ri_l = k - 4
        ri_r = k - 3
        ok_l = (ri_l >= 0) & (ri_l < n_ri)
        ok_r = (ri_r >= 0) & (ri_r < n_ri)
        base_l = (h * n_ri + jnp.clip(ri_l, 0, n_ri - 1)) * n_ci
        base_r = (h * n_ri + jnp.clip(ri_r, 0, n_ri - 1)) * n_ci
        acc = jnp.zeros((GRID_W, LANES), F32)
        for d in range(n_ci):
            v_l = jnp.where(ok_l, rpb_ref[base_l + d], 0.0)
            v_r = jnp.where(ok_r, rpb_ref[base_r + d], 0.0)
            acc = jnp.where(dd == d, jnp.where(hi_half, v_r, v_l), acc)
        tile = jnp.where(col_valid, acc * LOG2E, NEG_INF)
        o_ref[0, 0, k] = tile
        interior_l = (ri_l >= int_lo) & (ri_l < int_hi)
        interior_r = (ri_r >= int_lo) & (ri_r < int_hi)
        o_ref[1, 0, k] = jnp.where(hi_half, jnp.where(interior_r, tile, NEG_INF), jnp.where(interior_l, tile, NEG_INF))
        return carry

    lax.fori_loop(0, B_NK, body, 0)


def _bias_b(rpb):
    return pl.pallas_call(
        _bias_b_kernel,
        out_shape=jax.ShapeDtypeStruct((2, B_HEADS, B_NK, GRID_W, LANES), F32),
        grid=(B_HEADS,),
        in_specs=[pl.BlockSpec(memory_space=pltpu.SMEM)],
        out_specs=pl.BlockSpec((2, 1, B_NK, GRID_W, LANES), lambda h: (0, h, 0, 0, 0)),
        compiler_params=_params(1),
        name="bias_b",
    )(rpb.reshape(-1))


def _attn_b_kernel(q_ref, kvp_ref, kvc_ref, kvn_ref, qm_ref, mk_ref, mv_ref, pv_ref, o_ref):
    j = pl.program_id(1)
    last = pl.num_programs(1) - 1
    lo_mask = lax.broadcasted_iota(jnp.int32, (1, LANES), 1) < HEAD_DIM
    n_cols = B_KEY_ROWS // 2

    def kv_block(k, group):
        if k == 0:
            return kvp_ref[0, group]
        if k == B_SUBS + 1:
            return kvn_ref[0, group]
        return kvc_ref[0, group, B_TOK * (k - 1):B_TOK * k, :]

    def sub_group(sub, kind):
        first_block = {"interior": sub, "first": 1, "last": B_SUBS - 2}[kind]
        tok = slice(B_TOK * sub, B_TOK * (sub + 1))
        for grp in range(TOK_GROUPS):
            qg = q_ref[0, grp, tok, :]
            zero = jnp.zeros_like(qg)
            lhs = jnp.concatenate([jnp.where(lo_mask, qg, zero), jnp.where(lo_mask, zero, qg)], axis=0)
            kw = jnp.concatenate([kv_block(first_block + k, grp) for k in range(3)], axis=0)
            vw = jnp.concatenate([kv_block(first_block + k, TOK_GROUPS + grp) for k in range(3)], axis=0)
            s = _nt_dot(lhs, kw)
            blocks = []
            for hh in range(2):
                for rl in range(B_ROWS_PER_STEP):
                    if kind == "interior":
                        t0, t1, variant, k0 = rl // 2, (rl + NB_ROWS - 1) // 2 + 1, 1, NB_ROWS - 1 - rl
                    elif kind == "first":
                        t0, t1, variant, k0 = 0, NB_ROWS // 2, 0, NB_ROWS + 3 - rl
                    else:
                        t0, t1, variant, k0 = n_cols - NB_ROWS // 2, n_cols, 0, 3 - rl
                    r0 = hh * B_TOK + rl * GRID_W
                    bias = jnp.concatenate([pv_ref[variant, 2 * grp + hh, 2 * t + k0] for t in range(t0, t1)], axis=1)
                    sb = s[r0:r0 + GRID_W, LANES * t0:LANES * t1] + bias
                    e = jnp.exp2(sb - jnp.max(sb, axis=-1, keepdims=True)).astype(BF16)
                    parts = [jnp.zeros((GRID_W, LANES * t0), BF16)] if t0 else []
                    parts.append(e)
                    if t1 < n_cols:
                        parts.append(jnp.zeros((GRID_W, LANES * (n_cols - t1)), BF16))
                    blocks.append(jnp.concatenate(parts, axis=1))
            o, l = _pv_with_rowsum(jnp.concatenate(blocks, axis=0), vw)
            o = o / l
            o_ref[0, grp, tok, :] = jnp.where(lo_mask, o[:B_TOK], o[B_TOK:]).astype(BF16)

    def step(kinds):
        for sub, kind in enumerate(kinds):
            sub_group(sub, kind)
        _memory_heads(qm_ref, mk_ref, mv_ref, o_ref, lo_mask)

    @pl.when((j > 0) & (j < last))
    def _():
        step(("interior",) * B_SUBS)

    @pl.when(j == 0)
    def _():
        step(("first",) + ("interior",) * (B_SUBS - 1))

    @pl.when(j == last)
    def _():
        step(("interior",) * (B_SUBS - 1) + ("last",))


def _attn_b(proj, memkv, pv):
    batch, _, seq, _ = proj.shape
    step_tok = B_SUBS * B_TOK
    n_steps = seq // step_tok
    n_blocks = seq // B_TOK
    assert n_steps >= 2 and seq % step_tok == 0
    n_kv = 2 * TOK_GROUPS
    return pl.pallas_call(
        _attn_b_kernel,
        out_shape=jax.ShapeDtypeStruct((batch, CAT_GROUPS, seq, LANES), BF16),
        grid=(batch, n_steps),
        in_specs=[
            pl.BlockSpec((1, TOK_GROUPS, step_tok, LANES), lambda b, j: (b, 2, j, 0)),
            pl.BlockSpec((1, n_kv, B_TOK, LANES), lambda b, j: (b, 0, jnp.maximum(B_SUBS * j - 1, 0), 0)),
            pl.BlockSpec((1, n_kv, step_tok, LANES), lambda b, j: (b, 0, j, 0)),
            pl.BlockSpec((1, n_kv, B_TOK, LANES), lambda b, j: (b, 0, jnp.minimum(B_SUBS * (j + 1), n_blocks - 1), 0)),
            pl.BlockSpec((1, MEM_GROUPS, step_tok, LANES), lambda b, j: (b, 3 * TOK_GROUPS // MEM_GROUPS, j, 0)),
            pl.BlockSpec((1, MEM_GROUPS, MEM_LEN, LANES), lambda b, j: (b, 0, 0, 0)),
            pl.BlockSpec((1, MEM_GROUPS, MEM_LEN, LANES), lambda b, j: (b, 1, 0, 0)),
            _const_spec(pv.shape),
        ],
        out_specs=pl.BlockSpec((1, CAT_GROUPS, step_tok, LANES), lambda b, j: (b, 0, j, 0)),
        compiler_params=_params(2),
        name="attn_b",
    )(proj, proj, proj, proj, proj, memkv, memkv, pv)


def _post_kernel(x_ref, cat_ref, wo_ref, wgu_ref, wd_ref, gpost_ref, gpre_ref, gfpost_ref, o_ref):
    chain_rows = x_ref.shape[1] // POST_CHAINS
    chains = [slice(ch * chain_rows, (ch + 1) * chain_rows) for ch in range(POST_CHAINS)]

    def out_proj(rows):
        cat = jnp.concatenate([cat_ref[0, g, rows, :] for g in range(CAT_GROUPS)], axis=1)
        return jnp.dot(cat, wo_ref[...], preferred_element_type=F32)

    def ffn(h2):
        acc = jnp.zeros(h2.shape, F32)
        c0 = 0
        for size in FF_CHUNKS:
            cols = slice(c0, c0 + size)
            gate = jnp.dot(h2, wgu_ref[0, :, cols], preferred_element_type=F32)
            up = jnp.dot(h2, wgu_ref[1, :, cols], preferred_element_type=F32)
            act = (gate * jax.nn.sigmoid(gate) * up).astype(BF16)
            acc = acc + jnp.dot(act, wd_ref[cols, :], preferred_element_type=F32)
            c0 += size
        return acc

    mixed = [out_proj(rows) for rows in chains]
    for rows, mx in zip(chains, mixed):
        x1 = x_ref[0, rows, :] + _rms(mx, gpost_ref[...])
        h2 = _rms(x1, gpre_ref[...]).astype(BF16)
        o_ref[0, rows, :] = x1 + _rms(ffn(h2), gfpost_ref[...])


def _post(x, cat, w_out, w_gu, w_down, g_post, g_pre, g_fpost):
    batch, seq, _ = x.shape
    tm = min(TM_DENSE, seq)
    gain = lambda g: g.reshape(1, D_MODEL)
    x_spec = pl.BlockSpec((1, tm, D_MODEL), lambda b, i: (b, i, 0))
    return pl.pallas_call(
        _post_kernel,
        out_shape=jax.ShapeDtypeStruct(x.shape, F32),
        grid=(batch, seq // tm),
        in_specs=[
            x_spec,
            pl.BlockSpec((1, CAT_GROUPS, tm, LANES), lambda b, i: (b, 0, i, 0)),
            _const_spec(w_out.shape),
            _const_spec(w_gu.shape),
            _const_spec(w_down.shape),
            _const_spec((1, D_MODEL)),
            _const_spec((1, D_MODEL)),
            _const_spec((1, D_MODEL)),
        ],
        out_specs=x_spec,
        compiler_params=_params(2),
        name="post_ffn",
    )(x, cat, w_out, w_gu, w_down, gain(g_post), gain(g_pre), gain(g_fpost))


def _head_cols(order):
    return np.concatenate([np.arange(HEAD_DIM) + HEAD_DIM * h for h in order])


def _prep_w_in_a(w):
    q_cols = _head_cols(A_HEAD_ORDER)
    cols = np.concatenate([q_cols, np.arange(A_Q_HEADS * HEAD_DIM, w.shape[1])])
    return w[:, cols].astype(BF16)


def _prep_w_in_b(w):
    q_width = B_HEADS * HEAD_DIM
    cols = np.concatenate([np.arange(q_width, 3 * q_width), np.arange(q_width), np.arange(3 * q_width, w.shape[1])])
    return w[:, cols].astype(BF16)


def _prep_w_out_a(w):
    rows = np.concatenate([_head_cols(A_HEAD_ORDER), np.arange(A_Q_HEADS * HEAD_DIM, w.shape[0])])
    return w[rows, :].astype(BF16)


def _prep_w_gu(w):
    return jnp.stack([w[:, :D_FF], w[:, D_FF:]]).astype(BF16)


def kernel(x, mem, w_in_a, sink_a, w_in_b, rpb_b, t5_table, w_mem_kv, w_out, w_gu, w_down, norm_mix_pre,
           norm_mix_post, norm_mem, norm_ffn_pre, norm_ffn_post):
    depth = w_out.shape[0]
    memkv = _memkv(mem, norm_mem, w_mem_kv.astype(BF16))
    tb = _bias_a(t5_table)
    a_scaled = frozenset(range(TOK_GROUPS)) | frozenset(range(A_GROUPS - MEM_GROUPS, A_GROUPS))
    b_scaled = frozenset(range(2 * TOK_GROUPS, B_GROUPS))
    for i in range(depth):
        j = i // 2
        if i % 2 == 0:
            proj = _inproj(x, norm_mix_pre[i], _prep_w_in_a(w_in_a[j]), a_scaled)
            sink_tab = jnp.broadcast_to((sink_a[j].astype(F32) * LOG2E).reshape(A_KV_HEADS, A_GROUP, 1, 1),
                                        (A_KV_HEADS, A_GROUP, BLOCK, LANES)).reshape(A_KV_HEADS, A_GROUP * BLOCK, LANES)
            cat = _attn_a(proj, memkv[i], tb, sink_tab)
            wo = _prep_w_out_a(w_out[i])
        else:
            proj = _inproj(x, norm_mix_pre[i], _prep_w_in_b(w_in_b[j]), b_scaled)
            cat = _attn_b(proj, memkv[i], _bias_b(rpb_b[j]))
            wo = w_out[i].astype(BF16)
        x = _post(x, cat, wo, _prep_w_gu(w_gu[i]), w_down[i].astype(BF16),
                  norm_mix_post[i], norm_ffn_pre[i], norm_ffn_post[i])
    return x
```

```python
import functools
import math

import jax
import jax.numpy as jnp
import numpy as np
from jax import lax
from jax.experimental import pallas as pl
from jax.experimental.pallas import tpu as pltpu

F32 = jnp.float32
BF16 = jnp.bfloat16

D_MODEL = 1024
HEAD_DIM = 64
LANES = 128
A_Q_HEADS = 12
A_KV_HEADS = 4
A_GROUP = A_Q_HEADS // A_KV_HEADS
WINDOW = 128
BLOCK = 128
B_HEADS = 12
GRID_W = 64
NB_ROWS = 8
NB_COLS = 16
MEM_HEADS = 4
MEM_LEN = 256
N_BUCKETS = 32
MAX_DISTANCE = 128
D_FF = 2816
RMS_EPS = 1e-6
LOG2E = math.log2(math.e)
Q_SCALE = HEAD_DIM ** -0.5 * LOG2E

A_GROUPS = (A_Q_HEADS + 2 * A_KV_HEADS + MEM_HEADS) * HEAD_DIM // LANES
B_GROUPS = (3 * B_HEADS + MEM_HEADS) * HEAD_DIM // LANES
CAT_GROUPS = (A_Q_HEADS + MEM_HEADS) * HEAD_DIM // LANES
TOK_GROUPS = A_Q_HEADS * HEAD_DIM // LANES
MEM_GROUPS = MEM_HEADS * HEAD_DIM // LANES

A_HEAD_ORDER = tuple(3 * (2 * p + hh) + t for p in range(2) for t in range(3) for hh in range(2))

TM_DENSE = 1024
POST_CHAINS = 2
FF_CHUNKS = (256,) * 11
assert sum(FF_CHUNKS) == D_FF
A_TQ = 2048
B_ROWS_PER_STEP = 4
B_TOK = B_ROWS_PER_STEP * GRID_W
B_SUBS = 4
B_KEY_ROWS = 3 * B_ROWS_PER_STEP
B_NK = 22
NEG_INF = float("-inf")
VMEM_LIMIT = 56 * 1024 * 1024


def _rms(x, g):
    y = x * lax.rsqrt(jnp.mean(x * x, axis=-1, keepdims=True) + RMS_EPS)
    return y * g


def _nt_dot(a, b):
    return lax.dot_general(a, b, (((1,), (1,)), ((), ())), preferred_element_type=F32)


def _const_spec(shape):
    n = len(shape)
    return pl.BlockSpec(shape, lambda *_: (0,) * n, pipeline_mode=pl.Buffered(1))


def _params(n_axes, flags=None):
    return pltpu.CompilerParams(dimension_semantics=("arbitrary",) * n_axes, vmem_limit_bytes=VMEM_LIMIT,
                                flags=flags)


def _memkv_kernel(mem_ref, g_ref, w_ref, o_ref):
    batch = mem_ref.shape[0]
    m = _rms(mem_ref[...].reshape(batch * MEM_LEN, D_MODEL), g_ref[0]).astype(BF16)
    res = jnp.dot(m, w_ref[0], preferred_element_type=F32)
    for b in range(batch):
        for g in range(2 * MEM_GROUPS):
            o_ref[0, b, g] = res[b * MEM_LEN:(b + 1) * MEM_LEN, g * LANES:(g + 1) * LANES].astype(BF16)


def _memkv(mem, norm_mem, w_mem_kv):
    depth, batch = w_mem_kv.shape[0], mem.shape[0]
    return pl.pallas_call(
        _memkv_kernel,
        out_shape=jax.ShapeDtypeStruct((depth, batch, 2 * MEM_GROUPS, MEM_LEN, LANES), BF16),
        grid=(depth,),
        in_specs=[
            _const_spec(mem.shape),
            pl.BlockSpec((1, 1, D_MODEL), lambda l: (l, 0, 0)),
            pl.BlockSpec((1, D_MODEL, 2 * MEM_GROUPS * LANES), lambda l: (l, 0, 0)),
        ],
        out_specs=pl.BlockSpec((1, batch, 2 * MEM_GROUPS, MEM_LEN, LANES), lambda l: (l, 0, 0, 0, 0)),
        compiler_params=_params(1),
        name="mem_kv",
    )(mem, norm_mem.reshape(depth, 1, D_MODEL), w_mem_kv)


def _store_groups(o_ref, rows, res, scaled_groups):
    for g in range(o_ref.shape[1]):
        blk = res[:, g * LANES:(g + 1) * LANES]
        if g in scaled_groups:
            blk = blk * Q_SCALE
        o_ref[0, g, rows, :] = blk.astype(BF16)


def _inproj_kernel(x_ref, g_ref, w_ref, o_ref, *, scaled_groups):
    chain_rows = x_ref.shape[1] // POST_CHAINS
    for ch in range(POST_CHAINS):
        rows = slice(ch * chain_rows, (ch + 1) * chain_rows)
        h = _rms(x_ref[0, rows, :], g_ref[...]).astype(BF16)
        _store_groups(o_ref, rows, jnp.dot(h, w_ref[...], preferred_element_type=F32), scaled_groups)


def _inproj(x, gain, w, scaled_groups):
    batch, seq, _ = x.shape
    n_groups = w.shape[1] // LANES
    tm = min(TM_DENSE, seq)
    return pl.pallas_call(
        functools.partial(_inproj_kernel, scaled_groups=scaled_groups),
        out_shape=jax.ShapeDtypeStruct((batch, n_groups, seq, LANES), BF16),
        grid=(batch, seq // tm),
        in_specs=[
            pl.BlockSpec((1, tm, D_MODEL), lambda b, i: (b, i, 0)),
            _const_spec((1, D_MODEL)),
            _const_spec(w.shape),
        ],
        out_specs=pl.BlockSpec((1, n_groups, tm, LANES), lambda b, i: (b, 0, i, 0)),
        compiler_params=_params(2),
        name="in_proj",
    )(x, gain.reshape(1, D_MODEL), w)


def _split_halves(x, lo_mask):
    zero = jnp.zeros_like(x)
    return jnp.concatenate([jnp.where(lo_mask, x, zero), jnp.where(lo_mask, zero, x)], axis=0)


def _pv_with_rowsum(e_bf16, v):
    v_aug = jnp.concatenate([v, jnp.ones_like(v)], axis=1)
    o_aug = jnp.dot(e_bf16, v_aug, preferred_element_type=F32)
    return o_aug[:, :LANES], o_aug[:, LANES:]


def _memory_heads(qm_ref, mk_ref, mv_ref, o_ref, lo_mask):
    tq = qm_ref.shape[2]
    for grp in range(MEM_GROUPS):
        qg = qm_ref[0, grp]
        zero = jnp.zeros_like(qg)
        lhs = jnp.concatenate([jnp.where(lo_mask, qg, zero), jnp.where(lo_mask, zero, qg)], axis=0)
        s = _nt_dot(lhs, mk_ref[0, grp])
        m = jnp.max(s, axis=-1, keepdims=True)
        e = jnp.exp2(s - m).astype(BF16)
        o, l = _pv_with_rowsum(e, mv_ref[0, grp])
        o = o / l
        o_ref[0, TOK_GROUPS + grp] = jnp.where(lo_mask, o[:tq], o[tq:]).astype(BF16)


def _t5_bucket(rel):
    half = N_BUCKETS // 2
    max_exact = half // 2
    n = -rel
    ret = jnp.where(n < 0, half, 0)
    n = jnp.abs(n)
    nf = jnp.maximum(n, 1).astype(jnp.float32)
    large = max_exact + (jnp.log(nf / max_exact) / math.log(MAX_DISTANCE / max_exact)
                         * (half - max_exact)).astype(jnp.int32)
    large = jnp.minimum(large, half - 1)
    return ret + jnp.where(n < max_exact, n, large)


def _bias_a_kernel(t5_ref, bucket_ref, o_ref):
    q = lax.broadcasted_iota(jnp.int32, (BLOCK, BLOCK), 0)
    j = lax.broadcasted_iota(jnp.int32, (BLOCK, BLOCK), 1)
    masked = jnp.full((BLOCK, BLOCK), NEG_INF, F32)
    valid = (j >= q, None, j <= q)
    for g in range(A_KV_HEADS):
        for t in range(A_GROUP):
            head = A_GROUP * g + t
            rows = slice(BLOCK * t, BLOCK * (t + 1))
            for kb, variant in ((0, 0), (1, 2), (2, 3)):
                bucket = bucket_ref[kb]
                acc = jnp.zeros((BLOCK, BLOCK), F32)
                for b in range(N_BUCKETS):
                    acc = jnp.where(bucket == b, t5_ref[b * A_Q_HEADS + head], acc)
                acc = acc * LOG2E
                if valid[kb] is not None:
                    acc = jnp.where(valid[kb], acc, masked)
                o_ref[variant, g, rows, :] = acc
            o_ref[1, g, rows, :] = masked
            o_ref[4, g, rows, :] = masked


def _bias_a(t5_table):
    q = jnp.arange(BLOCK)[:, None]
    j = jnp.arange(BLOCK)[None, :]
    rel = jnp.stack([j - WINDOW - q, j - q, j + WINDOW - q])
    bucket = _t5_bucket(rel).astype(jnp.int32)
    return pl.pallas_call(
        _bias_a_kernel,
        out_shape=jax.ShapeDtypeStruct((5, A_KV_HEADS, A_GROUP * BLOCK, BLOCK), F32),
        in_specs=[pl.BlockSpec(memory_space=pltpu.SMEM), pl.BlockSpec(memory_space=pltpu.VMEM)],
        out_specs=pl.BlockSpec(memory_space=pltpu.VMEM),
        name="bias_a",
    )(t5_table.reshape(-1), bucket)


def _attn_a_kernel(q_ref, kp_ref, kc_ref, kn_ref, vp_ref, vc_ref, vn_ref, qm_ref, mk_ref, mv_ref, tb_ref,
                   sink_ref, o_ref):
    n_sub = q_ref.shape[2] // BLOCK
    i = pl.program_id(1)
    last = pl.num_programs(1) - 1
    first_prev_v = jnp.where(i > 0, 0, 1)
    last_next_v = jnp.where(i < last, 3, 4)
    lo_mask = lax.broadcasted_iota(jnp.int32, (1, LANES), 1) < HEAD_DIM
    n_keys = 3 * BLOCK
    lo_ones = jnp.where(lax.broadcasted_iota(jnp.int32, (n_keys, LANES), 1) < HEAD_DIM, 1.0, 0.0)
    ones_halves = jnp.concatenate([lo_ones, 1.0 - lo_ones], axis=0).astype(BF16)
    lo_f32 = lax.broadcasted_iota(jnp.int32, (A_GROUP * BLOCK, LANES), 1) < HEAD_DIM

    def key_blocks(prev_ref, cur_ref, next_ref, p):
        blocks = [prev_ref[0, p]] + [cur_ref[0, p, BLOCK * s:BLOCK * (s + 1), :] for s in range(n_sub)]
        blocks.append(next_ref[0, p])
        zero = jnp.zeros_like(blocks[0])
        return ([jnp.where(lo_mask, blk, zero) for blk in blocks], [jnp.where(lo_mask, zero, blk) for blk in blocks])

    def window(halves, s):
        return jnp.concatenate(halves[0][s:s + 3] + halves[1][s:s + 3], axis=0)

    k_halves = [key_blocks(kp_ref, kc_ref, kn_ref, p) for p in range(A_KV_HEADS // 2)]
    v_halves = [key_blocks(vp_ref, vc_ref, vn_ref, p) for p in range(A_KV_HEADS // 2)]

    def qk(p, s):
        rows = slice(BLOCK * s, BLOCK * (s + 1))
        lhs3 = jnp.concatenate([q_ref[0, A_GROUP * p + t, rows, :] for t in range(A_GROUP)], axis=0)
        return _nt_dot(lhs3, window(k_halves[p], s))

    def finish(p, s, scores):
        variants = (first_prev_v if s == 0 else 0, 2, last_next_v if s == n_sub - 1 else 3)
        es, sink_terms = [], []
        for hh in range(2):
            g = 2 * p + hh
            sc = [scores[:, n_keys * hh + LANES * j:n_keys * hh + LANES * (j + 1)] + tb_ref[variants[j], g]
                  for j in range(3)]
            sink = sink_ref[g]
            m = jnp.maximum(jnp.max(jnp.maximum(jnp.maximum(sc[0], sc[1]), sc[2]), axis=-1, keepdims=True), sink)
            es += [jnp.exp2(sj - m).astype(BF16) for sj in sc]
            sink_terms.append(jnp.exp2(sink - m))
        v_aug = jnp.concatenate([window(v_halves[p], s), ones_halves], axis=1)
        o_aug = jnp.dot(jnp.concatenate(es, axis=1), v_aug, preferred_element_type=F32)
        l = o_aug[:, LANES:] + jnp.where(lo_f32, sink_terms[0], sink_terms[1])
        o3 = o_aug[:, :LANES] / l
        for t in range(A_GROUP):
            o_ref[0, A_GROUP * p + t, BLOCK * s:BLOCK * (s + 1), :] = o3[BLOCK * t:BLOCK * (t + 1)].astype(BF16)

    for s in range(n_sub):
        for p in range(A_KV_HEADS // 2):
            finish(p, s, qk(p, s))
    _memory_heads(qm_ref, mk_ref, mv_ref, o_ref, lo_mask)


def _attn_a(proj, memkv, tb, sink_tab):
    batch, _, seq, _ = proj.shape
    tq = min(A_TQ, seq)
    n_sub = tq // BLOCK
    n_steps = seq // tq
    nblk = seq // BLOCK
    assert nblk >= 2

    def halo(group_block, first):
        idx = (lambda i: jnp.maximum(n_sub * i - 1, 0)) if first else (lambda i: jnp.minimum(n_sub * (i + 1), nblk - 1))
        return pl.BlockSpec((1, 2, BLOCK, LANES), lambda b, i: (b, group_block, idx(i), 0))

    def cur(group_block):
        return pl.BlockSpec((1, 2, tq, LANES), lambda b, i: (b, group_block, i, 0))

    return pl.pallas_call(
        _attn_a_kernel,
        out_shape=jax.ShapeDtypeStruct((batch, CAT_GROUPS, seq, LANES), BF16),
        grid=(batch, n_steps),
        in_specs=[
            pl.BlockSpec((1, TOK_GROUPS, tq, LANES), lambda b, i: (b, 0, i, 0)),
            halo(3, True), cur(3), halo(3, False),
            halo(4, True), cur(4), halo(4, False),
            pl.BlockSpec((1, MEM_GROUPS, tq, LANES), lambda b, i: (b, 5, i, 0)),
            pl.BlockSpec((1, MEM_GROUPS, MEM_LEN, LANES), lambda b, i: (b, 0, 0, 0)),
            pl.BlockSpec((1, MEM_GROUPS, MEM_LEN, LANES), lambda b, i: (b, 1, 0, 0)),
            _const_spec(tb.shape),
            _const_spec(sink_tab.shape),
        ],
        out_specs=pl.BlockSpec((1, CAT_GROUPS, tq, LANES), lambda b, i: (b, 0, i, 0)),
        compiler_params=_params(2),
        name="attn_a",
    )(proj, proj, proj, proj, proj, proj, proj, proj, memkv, memkv, tb, sink_tab)


def _bias_b_kernel(rpb_ref, o_ref):
    h = pl.program_id(0)
    n_ri = 2 * NB_ROWS - 1
    n_ci = 2 * NB_COLS - 1
    int_lo = NB_ROWS // 2 - 1
    int_hi = int_lo + NB_ROWS
    c = lax.broadcasted_iota(jnp.int32, (GRID_W, LANES), 0)
    lane = lax.broadcasted_iota(jnp.int32, (GRID_W, LANES), 1)
    cc = jnp.bitwise_and(lane, GRID_W - 1)
    hi_half = lane >= GRID_W
    cs = jnp.clip(c - NB_COLS // 2, 0, GRID_W - NB_COLS)
    col_valid = (cc >= cs) & (cc < cs + NB_COLS)
    dd = cc - c + NB_COLS - 1

    def body(k, carry):
        ri_l = k - 4
        ri_r = k - 3
        ok_l = (ri_l >= 0) & (ri_l < n_ri)
        ok_r = (ri_r >= 0) & (ri_r < n_ri)
        base_l = (h * n_ri + jnp.clip(ri_l, 0, n_ri - 1)) * n_ci
        base_r = (h * n_ri + jnp.clip(ri_r, 0, n_ri - 1)) * n_ci
        acc = jnp.zeros((GRID_W, LANES), F32)
        for d in range(n_ci):
            v_l = jnp.where(ok_l, rpb_ref[base_l + d], 0.0)
            v_r = jnp.where(ok_r, rpb_ref[base_r + d], 0.0)
            acc = jnp.where(dd == d, jnp.where(hi_half, v_r, v_l), acc)
        tile = jnp.where(col_valid, acc * LOG2E, NEG_INF)
        o_ref[0, 0, k] = tile
        interior_l = (ri_l >= int_lo) & (ri_l < int_hi)
        interior_r = (ri_r >= int_lo) & (ri_r < int_hi)
        o_ref[1, 0, k] = jnp.where(hi_half, jnp.where(interior_r, tile, NEG_INF), jnp.where(interior_l, tile, NEG_INF))
        return carry

    lax.fori_loop(0, B_NK, body, 0)


def _bias_b(rpb):
    return pl.pallas_call(
        _bias_b_kernel,
        out_shape=jax.ShapeDtypeStruct((2, B_HEADS, B_NK, GRID_W, LANES), F32),
        grid=(B_HEADS,),
        in_specs=[pl.BlockSpec(memory_space=pltpu.SMEM)],
        out_specs=pl.BlockSpec((2, 1, B_NK, GRID_W, LANES), lambda h: (0, h, 0, 0, 0)),
        compiler_params=_params(1),
        name="bias_b",
    )(rpb.reshape(-1))


def _attn_b_kernel(q_ref, kvp_ref, kvc_ref, kvn_ref, qm_ref, mk_ref, mv_ref, pv_ref, o_ref):
    j = pl.program_id(1)
    last = pl.num_programs(1) - 1
    lo_mask = lax.broadcasted_iota(jnp.int32, (1, LANES), 1) < HEAD_DIM
    n_cols = B_KEY_ROWS // 2

    def kv_block(k, group):
        if k == 0:
            return kvp_ref[0, group]
        if k == B_SUBS + 1:
            return kvn_ref[0, group]
        return kvc_ref[0, group, B_TOK * (k - 1):B_TOK * k, :]

    def sub_group(sub, kind):
        first_block = {"interior": sub, "first": 1, "last": B_SUBS - 2}[kind]
        tok = slice(B_TOK * sub, B_TOK * (sub + 1))
        for grp in range(TOK_GROUPS):
            qg = q_ref[0, grp, tok, :]
            zero = jnp.zeros_like(qg)
            lhs = jnp.concatenate([jnp.where(lo_mask, qg, zero), jnp.where(lo_mask, zero, qg)], axis=0)
            kw = jnp.concatenate([kv_block(first_block + k, grp) for k in range(3)], axis=0)
            vw = jnp.concatenate([kv_block(first_block + k, TOK_GROUPS + grp) for k in range(3)], axis=0)
            s = _nt_dot(lhs, kw)
            blocks = []
            for hh in range(2):
                for rl in range(B_ROWS_PER_STEP):
                    if kind == "interior":
                        t0, t1, variant, k0 = rl // 2, (rl + NB_ROWS - 1) // 2 + 1, 1, NB_ROWS - 1 - rl
                    elif kind == "first":
                        t0, t1, variant, k0 = 0, NB_ROWS // 2, 0, NB_ROWS + 3 - rl
                    else:
                        t0, t1, variant, k0 = n_cols - NB_ROWS // 2, n_cols, 0, 3 - rl
                    r0 = hh * B_TOK + rl * GRID_W
                    bias = jnp.concatenate([pv_ref[variant, 2 * grp + hh, 2 * t + k0] for t in range(t0, t1)], axis=1)
                    sb = s[r0:r0 + GRID_W, LANES * t0:LANES * t1] + bias
                    e = jnp.exp2(sb - jnp.max(sb, axis=-1, keepdims=True)).astype(BF16)
                    parts = [jnp.zeros((GRID_W, LANES * t0), BF16)] if t0 else []
                    parts.append(e)
                    if t1 < n_cols:
                        parts.append(jnp.zeros((GRID_W, LANES * (n_cols - t1)), BF16))
                    blocks.append(jnp.concatenate(parts, axis=1))
            o, l = _pv_with_rowsum(jnp.concatenate(blocks, axis=0), vw)
            o = o / l
            o_ref[0, grp, tok, :] = jnp.where(lo_mask, o[:B_TOK], o[B_TOK:]).astype(BF16)

    def step(kinds):
        for sub, kind in enumerate(kinds):
            sub_group(sub, kind)
        _memory_heads(qm_ref, mk_ref, mv_ref, o_ref, lo_mask)

    @pl.when((j > 0) & (j < last))
    def _():
        step(("interior",) * B_SUBS)

    @pl.when(j == 0)
    def _():
        step(("first",) + ("interior",) * (B_SUBS - 1))

    @pl.when(j == last)
    def _():
        step(("interior",) * (B_SUBS - 1) + ("last",))


def _attn_b(proj, memkv, pv):
    batch, _, seq, _ = proj.shape
    step_tok = B_SUBS * B_TOK
    n_steps = seq // step_tok
    n_blocks = seq // B_TOK
    assert n_steps >= 2 and seq % step_tok == 0
    n_kv = 2 * TOK_GROUPS
    return pl.pallas_call(
        _attn_b_kernel,
        out_shape=jax.ShapeDtypeStruct((batch, CAT_GROUPS, seq, LANES), BF16),
        grid=(batch, n_steps),
        in_specs=[
            pl.BlockSpec((1, TOK_GROUPS, step_tok, LANES), lambda b, j: (b, 2, j, 0)),
            pl.BlockSpec((1, n_kv, B_TOK, LANES), lambda b, j: (b, 0, jnp.maximum(B_SUBS * j - 1, 0), 0)),
            pl.BlockSpec((1, n_kv, step_tok, LANES), lambda b, j: (b, 0, j, 0)),
            pl.BlockSpec((1, n_kv, B_TOK, LANES), lambda b, j: (b, 0, jnp.minimum(B_SUBS * (j + 1), n_blocks - 1), 0)),
            pl.BlockSpec((1, MEM_GROUPS, step_tok, LANES), lambda b, j: (b, 3 * TOK_GROUPS // MEM_GROUPS, j, 0)),
            pl.BlockSpec((1, MEM_GROUPS, MEM_LEN, LANES), lambda b, j: (b, 0, 0, 0)),
            pl.BlockSpec((1, MEM_GROUPS, MEM_LEN, LANES), lambda b, j: (b, 1, 0, 0)),
            _const_spec(pv.shape),
        ],
        out_specs=pl.BlockSpec((1, CAT_GROUPS, step_tok, LANES), lambda b, j: (b, 0, j, 0)),
        compiler_params=_params(2),
        name="attn_b",
    )(proj, proj, proj, proj, proj, memkv, memkv, pv)


def _post_kernel(x_ref, cat_ref, wo_ref, wgu_ref, wd_ref, gpost_ref, gpre_ref, gfpost_ref, o_ref):
    chain_rows = x_ref.shape[1] // POST_CHAINS
    chains = [slice(ch * chain_rows, (ch + 1) * chain_rows) for ch in range(POST_CHAINS)]

    def out_proj(rows):
        cat = jnp.concatenate([cat_ref[0, g, rows, :] for g in range(CAT_GROUPS)], axis=1)
        return jnp.dot(cat, wo_ref[...], preferred_element_type=F32)

    def ffn(h2):
        acc = jnp.zeros(h2.shape, F32)
        c0 = 0
        for size in FF_CHUNKS:
            cols = slice(c0, c0 + size)
            gate = jnp.dot(h2, wgu_ref[0, :, cols], preferred_element_type=F32)
            up = jnp.dot(h2, wgu_ref[1, :, cols], preferred_element_type=F32)
            act = (gate * jax.nn.sigmoid(gate) * up).astype(BF16)
            acc = acc + jnp.dot(act, wd_ref[cols, :], preferred_element_type=F32)
            c0 += size
        return acc

    mixed = [out_proj(rows) for rows in chains]
    for rows, mx in zip(chains, mixed):
        x1 = x_ref[0, rows, :] + _rms(mx, gpost_ref[...])
        h2 = _rms(x1, gpre_ref[...]).astype(BF16)
        o_ref[0, rows, :] = x1 + _rms(ffn(h2), gfpost_ref[...])


def _post(x, cat, w_out, w_gu, w_down, g_post, g_pre, g_fpost):
    batch, seq, _ = x.shape
    tm = min(TM_DENSE, seq)
    gain = lambda g: g.reshape(1, D_MODEL)
    x_spec = pl.BlockSpec((1, tm, D_MODEL), lambda b, i: (b, i, 0))
    return pl.pallas_call(
        _post_kernel,
        out_shape=jax.ShapeDtypeStruct(x.shape, F32),
        grid=(batch, seq // tm),
        in_specs=[
            x_spec,
            pl.BlockSpec((1, CAT_GROUPS, tm, LANES), lambda b, i: (b, 0, i, 0)),
            _const_spec(w_out.shape),
            _const_spec(w_gu.shape),
            _const_spec(w_down.shape),
            _const_spec((1, D_MODEL)),
            _const_spec((1, D_MODEL)),
            _const_spec((1, D_MODEL)),
        ],
        out_specs=x_spec,
        compiler_params=_params(2),
        name="post_ffn",
    )(x, cat, w_out, w_gu, w_down, gain(g_post), gain(g_pre), gain(g_fpost))


def _head_cols(order):
    return np.concatenate([np.arange(HEAD_DIM) + HEAD_DIM * h for h in order])


def _prep_w_in_a(w):
    q_cols = _head_cols(A_HEAD_ORDER)
    cols = np.concatenate([q_cols, np.arange(A_Q_HEADS * HEAD_DIM, w.shape[1])])
    return w[:, cols].astype(BF16)


def _prep_w_in_b(w):
    q_width = B_HEADS * HEAD_DIM
    cols = np.concatenate([np.arange(q_width, 3 * q_width), np.arange(q_width), np.arange(3 * q_width, w.shape[1])])
    return w[:, cols].astype(BF16)


def _prep_w_out_a(w):
    rows = np.concatenate([_head_cols(A_HEAD_ORDER), np.arange(A_Q_HEADS * HEAD_DIM, w.shape[0])])
    return w[rows, :].astype(BF16)


def _prep_w_gu(w):
    return jnp.stack([w[:, :D_FF], w[:, D_FF:]]).astype(BF16)


def kernel(x, mem, w_in_a, sink_a, w_in_b, rpb_b, t5_table, w_mem_kv, w_out, w_gu, w_down, norm_mix_pre,
           norm_mix_post, norm_mem, norm_ffn_pre, norm_ffn_post):
    depth = w_out.shape[0]
    memkv = _memkv(mem, norm_mem, w_mem_kv.astype(BF16))
    tb = _bias_a(t5_table)
    a_scaled = frozenset(range(TOK_GROUPS)) | frozenset(range(A_GROUPS - MEM_GROUPS, A_GROUPS))
    b_scaled = frozenset(range(2 * TOK_GROUPS, B_GROUPS))
    for i in range(depth):
        j = i // 2
        if i % 2 == 0:
            proj = _inproj(x, norm_mix_pre[i], _prep_w_in_a(w_in_a[j]), a_scaled)
            sink_tab = jnp.broadcast_to((sink_a[j].astype(F32) * LOG2E).reshape(A_KV_HEADS, A_GROUP, 1, 1),
                                        (A_KV_HEADS, A_GROUP, BLOCK, LANES)).reshape(A_KV_HEADS, A_GROUP * BLOCK, LANES)
            cat = _attn_a(proj, memkv[i], tb, sink_tab)
            wo = _prep_w_out_a(w_out[i])
        else:
            proj = _inproj(x, norm_mix_pre[i], _prep_w_in_b(w_in_b[j]), b_scaled)
            cat = _attn_b(proj, memkv[i], _bias_b(rpb_b[j]))
            wo = w_out[i].astype(BF16)
        x = _post(x, cat, wo, _prep_w_gu(w_gu[i]), w_down[i].astype(BF16),
                  norm_mix_post[i], norm_ffn_pre[i], norm_ffn_post[i])
    return x
```
